```python
import math
import jax, jax.numpy as jnp
from jax import lax
import numpy as np

D_MODEL = 2048
BATCH = 2
SEQ = 4096
DEPTH = 2

D_FF = 5632
POOL_WIDTH = D_MODEL // 2
POOL_WINDOWS = (2, 4, 8, 16)
POOL_GROUP = POOL_WIDTH // len(POOL_WINDOWS)
CONV_WIDTH_CH = D_MODEL // 2
CONV_K = 3
MIX_IN = POOL_WIDTH + 3 * CONV_WIDTH_CH
MIX_OUT = POOL_WIDTH + CONV_WIDTH_CH
N_HEADS = 16
HEAD_DIM = D_MODEL // N_HEADS
Q_BLOCK = 128
LN_EPS = 1e-5
N_EVEN = (DEPTH + 1) // 2
N_ODD = DEPTH // 2
DEEPNORM_ALPHA = (2.0 * DEPTH) ** 0.25
DEEPNORM_BETA = (8.0 * DEPTH) ** -0.25

kernel_name = "hybrid_pool_shortconv_stickbreak_macaron_deepnorm"


def layer_norm(x, g, b):
    xf = x.astype(jnp.float32)
    mu = jnp.mean(xf, axis=-1, keepdims=True)
    var = jnp.mean(jnp.square(xf - mu), axis=-1, keepdims=True)
    y = (xf - mu) * lax.rsqrt(var + LN_EPS)
    return (y * g.astype(jnp.float32) + b.astype(jnp.float32)).astype(x.dtype)


def swiglu(x, w_gate, w_up, w_down):
    return (jax.nn.silu(x @ w_gate) * (x @ w_up)) @ w_down


def pool_mixer(u, w_groups, scale):
    S = u.shape[1]
    uf = u.astype(jnp.float32)
    cs = jnp.cumsum(uf, axis=1)
    count_full = jnp.arange(1, S + 1, dtype=jnp.float32)[None, :, None]
    outs = []
    for g, win in enumerate(POOL_WINDOWS):
        sl = slice(g * POOL_GROUP, (g + 1) * POOL_GROUP)
        c = cs[..., sl]
        c_prev = jnp.pad(c, ((0, 0), (win, 0), (0, 0)))[:, :S]
        mean = (c - c_prev) / jnp.minimum(count_full, float(win))
        d = (mean - uf[..., sl]).astype(u.dtype)
        outs.append(d @ w_groups[g])
    return jnp.concatenate(outs, axis=-1) * scale


def short_gated_conv(xin, gate_b, gate_c, conv_w):
    z = gate_c * xin
    ch = z.shape[-1]
    y = lax.conv_general_dilated(
        z, conv_w[:, None, :].astype(z.dtype), window_strides=(1,),
        padding=[(CONV_K - 1, 0)], dimension_numbers=("NWC", "WIO", "NWC"),
        feature_group_count=ch)
    return gate_b * y


def stick_breaking_attention(q, k, v):
    S = q.shape[2]
    scale = 1.0 / math.sqrt(HEAD_DIM)
    outs = []
    for blk in range(S // Q_BLOCK):
        q0 = blk * Q_BLOCK
        kv_len = q0 + Q_BLOCK
        qb = q[:, :, q0:kv_len]
        kb = k[:, :, :kv_len]
        vb = v[:, :, :kv_len]
        z = jnp.einsum("bhqd,bhkd->bhqk", qb, kb).astype(jnp.float32) * scale
        t_idx = q0 + jnp.arange(Q_BLOCK)[:, None]
        s_idx = jnp.arange(kv_len)[None, :]
        mask = s_idx < t_idx
        log_1m_beta = jnp.where(mask, jax.nn.log_sigmoid(-z), 0.0)
        suffix = lax.cumsum(log_1m_beta, axis=3, reverse=True) - log_1m_beta
        log_a = jax.nn.log_sigmoid(z) + suffix
        a = jnp.where(mask, jnp.exp(log_a), 0.0)
        outs.append(jnp.einsum("bhqk,bhkd->bhqd", a.astype(vb.dtype), vb))
    return jnp.concatenate(outs, axis=2)


def setup_inputs(seed: int = 0) -> dict:
    key = jax.random.key(seed)
    ks = jax.random.split(key, 16)
    f32 = jnp.float32
    nrm = lambda k, shape, s: jax.random.normal(k, shape, f32) * s
    x = jax.random.normal(ks[0], (BATCH, SEQ, D_MODEL), f32)
    ln_g = 1.0 + nrm(ks[1], (DEPTH, 3, D_MODEL), 0.02)
    ln_b = nrm(ks[2], (DEPTH, 3, D_MODEL), 0.02)
    ffn_w_gate = nrm(ks[3], (DEPTH, 2, D_MODEL, D_FF), D_MODEL ** -0.5)
    ffn_w_up = nrm(ks[4], (DEPTH, 2, D_MODEL, D_FF), D_MODEL ** -0.5)
    ffn_w_down = nrm(ks[5], (DEPTH, 2, D_FF, D_MODEL), D_FF ** -0.5 * DEEPNORM_BETA)
    mix_w_in = nrm(ks[6], (N_EVEN, D_MODEL, MIX_IN), D_MODEL ** -0.5)
    pool_w = nrm(ks[7], (N_EVEN, len(POOL_WINDOWS), POOL_GROUP, POOL_GROUP), POOL_GROUP ** -0.5)
    pool_scale = 1.0 + nrm(ks[8], (N_EVEN, POOL_WIDTH), 0.02)
    conv_w = nrm(ks[9], (N_EVEN, CONV_K, CONV_WIDTH_CH), CONV_K ** -0.5)
    mix_w_out = nrm(ks[10], (N_EVEN, MIX_OUT, D_MODEL), MIX_OUT ** -0.5 * DEEPNORM_BETA)
    attn_w_qkv = nrm(ks[11], (N_ODD, D_MODEL, 3 * D_MODEL), D_MODEL ** -0.5)
    attn_w_out = nrm(ks[12], (N_ODD, D_MODEL, D_MODEL), D_MODEL ** -0.5 * DEEPNORM_BETA)
    return {"x": x, "ln_g": ln_g, "ln_b": ln_b, "ffn_w_gate": ffn_w_gate,
            "ffn_w_up": ffn_w_up, "ffn_w_down": ffn_w_down, "mix_w_in": mix_w_in,
            "pool_w": pool_w, "pool_scale": pool_scale, "conv_w": conv_w,
            "mix_w_out": mix_w_out, "attn_w_qkv": attn_w_qkv, "attn_w_out": attn_w_out}


def reference(x, ln_g, ln_b, ffn_w_gate, ffn_w_up, ffn_w_down, mix_w_in, pool_w,
              pool_scale, conv_w, mix_w_out, attn_w_qkv, attn_w_out):
    B, S, D = x.shape
    a = DEEPNORM_ALPHA
    for layer in range(DEPTH):
        f = swiglu(x, ffn_w_gate[layer, 0], ffn_w_up[layer, 0], ffn_w_down[layer, 0])
        x = layer_norm(a * x + 0.5 * f, ln_g[layer, 0], ln_b[layer, 0])
        i = layer // 2
        if layer % 2 == 0:
            h = x @ mix_w_in[i]
            u_pool = h[..., :POOL_WIDTH]
            o0 = POOL_WIDTH
            gate_b = h[..., o0:o0 + CONV_WIDTH_CH]
            gate_c = h[..., o0 + CONV_WIDTH_CH:o0 + 2 * CONV_WIDTH_CH]
            x_conv = h[..., o0 + 2 * CONV_WIDTH_CH:]
            y_pool = pool_mixer(u_pool, pool_w[i], pool_scale[i])
            y_conv = short_gated_conv(x_conv, gate_b, gate_c, conv_w[i])
            m = jnp.concatenate([y_pool, y_conv], axis=-1) @ mix_w_out[i]
        else:
            qkv = (x @ attn_w_qkv[i]).reshape(B, S, 3, N_HEADS, HEAD_DIM)
            qkv = jnp.transpose(qkv, (2, 0, 3, 1, 4))
            o = stick_breaking_attention(qkv[0], qkv[1], qkv[2])
            o = jnp.transpose(o, (0, 2, 1, 3)).reshape(B, S, D)
            m = o @ attn_w_out[i]
        x = layer_norm(a * x + m, ln_g[layer, 1], ln_b[layer, 1])
        f = swiglu(x, ffn_w_gate[layer, 1], ffn_w_up[layer, 1], ffn_w_down[layer, 1])
        x = layer_norm(a * x + 0.5 * f, ln_g[layer, 2], ln_b[layer, 2])
    return x
```

```python
import functools
import math

import jax
import jax.numpy as jnp
from jax import lax
from jax.experimental import pallas as pl
from jax.experimental.pallas import tpu as pltpu

F32 = jnp.float32
BF16 = jnp.bfloat16

DEPTH = 2
LN_EPS = 1e-5
DEEPNORM_ALPHA = (2.0 * DEPTH) ** 0.25
N_HEADS = 16
POOL_WINDOWS = (2, 4, 8, 16)
CONV_K = 3

V7X_VMEM_BYTES = 64 * 1024 * 1024
VMEM_LIMIT = V7X_VMEM_BYTES - 6 * 1024 * 1024

ROW_TILE = 1024
FF_TILE = 256
LN_ROWS = 256
PROJ_K_TILE = 512
PROJ_ROW_TILE = 512
QKV_COL_TILE = 1024
ATTN_TILE = 256
POOL_HALO = 32
CONV_HALO = 8


def _params(semantics):
    return pltpu.CompilerParams(dimension_semantics=semantics, vmem_limit_bytes=VMEM_LIMIT)


def _layer_norm_rows(y, g, b):
    mu = jnp.mean(y, axis=-1, keepdims=True)
    yc = y - mu
    var = jnp.mean(yc * yc, axis=-1, keepdims=True)
    return yc * lax.rsqrt(var + LN_EPS) * g + b


def _residual_ln_epilogue(x_ref, acc_ref, g_ref, b_ref, o_ref, branch_scale):
    g = g_ref[...]
    b = b_ref[...]
    for r in range(0, x_ref.shape[0], LN_ROWS):
        rows = pl.ds(r, LN_ROWS)
        y = DEEPNORM_ALPHA * x_ref[rows, :] + branch_scale * acc_ref[rows, :]
        o_ref[rows, :] = _layer_norm_rows(y, g, b)


def _ffn_kernel(x_ref, wg_ref, wu_ref, wd_ref, g_ref, b_ref, o_ref, xb_ref):
    j = pl.program_id(1)

    @pl.when(j == 0)
    def _():
        xb_ref[...] = x_ref[...].astype(BF16)

    xb = xb_ref[...]
    gate = jnp.dot(xb, wg_ref[...].astype(BF16), preferred_element_type=F32)
    up = jnp.dot(xb, wu_ref[...].astype(BF16), preferred_element_type=F32)
    h = (gate * jax.nn.sigmoid(gate) * up).astype(BF16)
    down = jnp.dot(h, wd_ref[...].astype(BF16), preferred_element_type=F32)

    @pl.when(j == 0)
    def _():
        o_ref[...] = down

    @pl.when(j > 0)
    def _():
        o_ref[...] += down

    @pl.when(j == pl.num_programs(1) - 1)
    def _():
        _residual_ln_epilogue(x_ref, o_ref, g_ref, b_ref, o_ref, 0.5)


def _ffn_ln(x, wg, wu, wd, layer, idx, g, b):
    m, d = x.shape
    f = wg.shape[-1]
    grid = (m // ROW_TILE, f // FF_TILE)
    return pl.pallas_call(
        _ffn_kernel,
        name="ffn_ln",
        grid=grid,
        in_specs=[
            pl.BlockSpec((ROW_TILE, d), lambda i, j: (i, 0), pipeline_mode=pl.Buffered(1)),
            pl.BlockSpec((None, None, d, FF_TILE), lambda i, j: (layer, idx, 0, j)),
            pl.BlockSpec((None, None, d, FF_TILE), lambda i, j: (layer, idx, 0, j)),
            pl.BlockSpec((None, None, FF_TILE, d), lambda i, j: (layer, idx, j, 0)),
            pl.BlockSpec((1, d), lambda i, j: (0, 0)),
            pl.BlockSpec((1, d), lambda i, j: (0, 0)),
        ],
        out_specs=pl.BlockSpec((ROW_TILE, d), lambda i, j: (i, 0)),
        out_shape=jax.ShapeDtypeStruct((m, d), F32),
        scratch_shapes=[pltpu.VMEM((ROW_TILE, d), BF16)],
        compiler_params=_params(("arbitrary", "arbitrary")),
    )(x, wg, wu, wd, g, b)


def _proj_kernel(a0_ref, a1_ref, w_ref, x_ref, g_ref, b_ref, o_ref, *, half_steps):
    k = pl.program_id(1)
    w = w_ref[...].astype(BF16)

    @pl.when(k == 0)
    def _():
        o_ref[...] = jnp.dot(a0_ref[...], w, preferred_element_type=F32)

    @pl.when(jnp.logical_and(k > 0, k < half_steps))
    def _():
        o_ref[...] += jnp.dot(a0_ref[...], w, preferred_element_type=F32)

    @pl.when(k >= half_steps)
    def _():
        o_ref[...] += jnp.dot(a1_ref[...], w, preferred_element_type=F32)

    @pl.when(k == pl.num_programs(1) - 1)
    def _():
        _residual_ln_epilogue(x_ref, o_ref, g_ref, b_ref, o_ref, 1.0)


def _proj_ln(a0, a1, col0, col1, w, x, g, b):
    m, d = x.shape
    kdim = w.shape[0]
    half = kdim // 2
    half_steps = half // PROJ_K_TILE
    grid = (m // PROJ_ROW_TILE, kdim // PROJ_K_TILE)
    a0_map = lambda i, k: (i, col0 * half_steps + jnp.minimum(k, half_steps - 1))
    a1_map = lambda i, k: (i, col1 * half_steps + jnp.maximum(k - half_steps, 0))
    return pl.pallas_call(
        functools.partial(_proj_kernel, half_steps=half_steps),
        name="proj_ln",
        grid=grid,
        in_specs=[
            pl.BlockSpec((PROJ_ROW_TILE, PROJ_K_TILE), a0_map),
            pl.BlockSpec((PROJ_ROW_TILE, PROJ_K_TILE), a1_map),
            pl.BlockSpec((PROJ_K_TILE, d), lambda i, k: (k, 0)),
            pl.BlockSpec((PROJ_ROW_TILE, d), lambda i, k: (i, 0)),
            pl.BlockSpec((1, d), lambda i, k: (0, 0)),
            pl.BlockSpec((1, d), lambda i, k: (0, 0)),
        ],
        out_specs=pl.BlockSpec((PROJ_ROW_TILE, d), lambda i, k: (i, 0)),
        out_shape=jax.ShapeDtypeStruct((m, d), F32),
        compiler_params=_params(("arbitrary", "arbitrary")),
    )(a0, a1, w, x, g, b)


def _mixer_kernel(x_ref, wp_ref, wb_ref, wc_ref, wx_ref, pw_ref, ps_ref, cw_ref,
                  yp_ref, yc_ref, xb_ref, pool_ref, conv_ref, pcarry_ref, ccarry_ref,
                  *, tiles_per_seq):
    i = pl.program_id(0)
    j = pl.program_id(1)
    tm = x_ref.shape[0]
    seq_tile = i % tiles_per_seq

    @pl.when(j == 0)
    def _():
        xb_ref[...] = x_ref[...].astype(BF16)

    @pl.when(seq_tile == 0)
    def _():
        pcarry_ref[j] = jnp.zeros(pcarry_ref.shape[1:], F32)
        ccarry_ref[j] = jnp.zeros(ccarry_ref.shape[1:], F32)

    xb = xb_ref[...]
    u = jnp.dot(xb, wp_ref[...].astype(BF16), preferred_element_type=F32)
    gate_b = jnp.dot(xb, wb_ref[...].astype(BF16), preferred_element_type=F32)
    gate_c = jnp.dot(xb, wc_ref[...].astype(BF16), preferred_element_type=F32)
    x_conv = jnp.dot(xb, wx_ref[...].astype(BF16), preferred_element_type=F32)

    base = 8 + POOL_HALO
    pool_ref[pl.ds(8, POOL_HALO), :] = pcarry_ref[j]
    pool_ref[pl.ds(base, tm), :] = u
    pcarry_ref[j] = u[tm - POOL_HALO:, :]
    for step in range(len(POOL_WINDOWS)):
        shift = 1 << step
        lo = 16 + 8 * step
        n = base + tm - lo

        @pl.when(j >= step)
        def _():
            pool_ref[pl.ds(lo, n), :] = (pool_ref[pl.ds(lo, n), :]
                                         + pool_ref[pl.ds(lo - shift, n), :])

    window = lax.shift_left(jnp.int32(2), j)
    pos = seq_tile * tm + lax.broadcasted_iota(jnp.int32, (tm, 1), 0)
    count = jnp.minimum(pos + 1, window).astype(F32)
    d = pool_ref[pl.ds(base, tm), :] / count - u
    y_pool = jnp.dot(d.astype(BF16), pw_ref[0].astype(BF16), preferred_element_type=F32)
    yp_ref[...] = (y_pool * ps_ref[...]).astype(yp_ref.dtype)

    z = gate_c * x_conv
    cbase = CONV_HALO
    conv_ref[pl.ds(0, CONV_HALO), :] = ccarry_ref[j]
    conv_ref[pl.ds(cbase, tm), :] = z
    ccarry_ref[j] = z[tm - CONV_HALO:, :]
    cw = cw_ref[...]
    y = z * cw[CONV_K - 1:CONV_K, :]
    for tap in range(CONV_K - 1):
        back = CONV_K - 1 - tap
        y = y + conv_ref[pl.ds(cbase - back, tm), :] * cw[tap:tap + 1, :]
    yc_ref[...] = (gate_b * y).astype(yc_ref.dtype)


def _mixer_front(x, w_in, pool_w, pool_scale, conv_w, seq_len):
    m, d = x.shape
    n_groups, group, _ = pool_w.shape
    pool_width = n_groups * group
    conv_width = conv_w.shape[1]
    assert conv_width == pool_width and w_in.shape[1] == pool_width + 3 * conv_width
    assert tuple(2 << g for g in range(n_groups)) == POOL_WINDOWS
    assert POOL_HALO >= POOL_WINDOWS[-1] - 1 and seq_len % ROW_TILE == 0
    chunk = group
    nb = pool_width // chunk
    grid = (m // ROW_TILE, n_groups)
    w_spec = lambda seg: pl.BlockSpec((d, chunk), lambda i, j: (0, seg * nb + j))
    out_spec = pl.BlockSpec((ROW_TILE, chunk), lambda i, j: (i, j))
    return pl.pallas_call(
        functools.partial(_mixer_kernel, tiles_per_seq=seq_len // ROW_TILE),
        name="mixer_front",
        grid=grid,
        in_specs=[
            pl.BlockSpec((ROW_TILE, d), lambda i, j: (i, 0)),
            w_spec(0), w_spec(1), w_spec(2), w_spec(3),
            pl.BlockSpec((1, group, group), lambda i, j: (j, 0, 0)),
            pl.BlockSpec((1, chunk), lambda i, j: (0, j)),
            pl.BlockSpec((CONV_K, chunk), lambda i, j: (0, j)),
        ],
        out_specs=[out_spec, out_spec],
        out_shape=[jax.ShapeDtypeStruct((m, pool_width), BF16),
                   jax.ShapeDtypeStruct((m, conv_width), BF16)],
        scratch_shapes=[
            pltpu.VMEM((ROW_TILE, d), BF16),
            pltpu.VMEM((8 + POOL_HALO + ROW_TILE, chunk), F32),
            pltpu.VMEM((CONV_HALO + ROW_TILE, chunk), F32),
            pltpu.VMEM((n_groups, POOL_HALO, chunk), F32),
            pltpu.VMEM((n_groups, CONV_HALO, chunk), F32),
        ],
        compiler_params=_params(("arbitrary", "arbitrary")),
    )(x, w_in, w_in, w_in, w_in, pool_w, pool_scale, conv_w)


def _qkv_kernel(x_ref, w_ref, o_ref, xb_ref):
    @pl.when(pl.program_id(1) == 0)
    def _():
        xb_ref[...] = x_ref[...].astype(BF16)

    o_ref[...] = jnp.dot(xb_ref[...], w_ref[...].astype(BF16),
                         preferred_element_type=F32).astype(o_ref.dtype)


def _qkv_proj(x, w):
    m, d = x.shape
    n = w.shape[1]
    grid = (m // ROW_TILE, n // QKV_COL_TILE)
    return pl.pallas_call(
        _qkv_kernel,
        name="qkv_proj",
        grid=grid,
        in_specs=[
            pl.BlockSpec((ROW_TILE, d), lambda i, j: (i, 0)),
            pl.BlockSpec((d, QKV_COL_TILE), lambda i, j: (0, j)),
        ],
        out_specs=pl.BlockSpec((ROW_TILE, QKV_COL_TILE), lambda i, j: (i, j)),
        out_shape=jax.ShapeDtypeStruct((m, n), BF16),
        scratch_shapes=[pltpu.VMEM((ROW_TILE, d), BF16)],
        compiler_params=_params(("arbitrary", "arbitrary")),
    )(x, w)


def _attn_kernel(q_ref, k_ref, v_ref, tri_ref, o_ref, acc_ref, carry_ref, *, scale):
    seq = q_ref.shape[0]
    t = ATTN_TILE

    def block(q, kb, masked):
        rows = pl.ds(pl.multiple_of(kb * t, t), t)
        z = lax.dot_general(q, k_ref[rows, :], (((1,), (1,)), ((), ())),
                            preferred_element_type=F32) * scale
        sp = jnp.maximum(z, 0.0) + jnp.log(1.0 + jnp.exp(-jnp.abs(z)))
        if masked:
            keep = (lax.broadcasted_iota(jnp.int32, (t, t), 1)
                    < lax.broadcasted_iota(jnp.int32, (t, t), 0))
            sp = jnp.where(keep, sp, 0.0)
        hi = sp.astype(BF16)
        lo = (sp - hi.astype(F32)).astype(BF16)
        suffix = jnp.dot(jnp.concatenate([hi, lo], axis=1), tri_ref[...],
                         preferred_element_type=F32)
        a = jnp.exp(z - suffix - carry_ref[...])
        if masked:
            a = jnp.where(keep, a, 0.0)
        acc_ref[...] += jnp.dot(a.astype(BF16), v_ref[rows, :], preferred_element_type=F32)
        carry_ref[...] += suffix[:, 0:1]

    def q_tile(qi, _):
        q = q_ref[pl.ds(pl.multiple_of(qi * t, t), t), :]
        acc_ref[...] = jnp.zeros(acc_ref.shape, F32)
        carry_ref[...] = jnp.zeros(carry_ref.shape, F32)
        block(q, qi, True)

        def off_diagonal(it, _):
            block(q, qi - 1 - it, False)
            return 0

        lax.fori_loop(0, qi, off_diagonal, 0)
        o_ref[pl.ds(pl.multiple_of(qi * t, t), t), :] = acc_ref[...].astype(o_ref.dtype)
        return 0

    lax.fori_loop(0, seq // t, q_tile, 0)


def _attention(qkv, batch, seq_len, head_dim):
    m = qkv.shape[0]
    t = ATTN_TILE
    tri = (lax.broadcasted_iota(jnp.int32, (t, t), 0)
           >= lax.broadcasted_iota(jnp.int32, (t, t), 1)).astype(BF16)
    tri2 = jnp.concatenate([tri, tri], axis=0)
    head_spec = lambda part: pl.BlockSpec((seq_len, head_dim),
                                          lambda b, h: (b, part * N_HEADS + h))
    return pl.pallas_call(
        functools.partial(_attn_kernel, scale=1.0 / math.sqrt(head_dim)),
        name="stickbreak_attn",
        grid=(batch, N_HEADS),
        in_specs=[head_spec(0), head_spec(1), head_spec(2),
                  pl.BlockSpec((2 * t, t), lambda b, h: (0, 0))],
        out_specs=pl.BlockSpec((seq_len, head_dim), lambda b, h: (b, h)),
        out_shape=jax.ShapeDtypeStruct((m, N_HEADS * head_dim), BF16),
        scratch_shapes=[pltpu.VMEM((t, head_dim), F32), pltpu.VMEM((t, 1), F32)],
        compiler_params=_params(("arbitrary", "arbitrary")),
    )(qkv, qkv, qkv, tri2)


def kernel(x, ln_g, ln_b, ffn_w_gate, ffn_w_up, ffn_w_down, mix_w_in, pool_w, pool_scale,
           conv_w, mix_w_out, attn_w_qkv, attn_w_out):
    batch, seq_len, d = x.shape
    assert ln_g.shape[0] == DEPTH
    h = x.reshape(batch * seq_len, d)
    row = lambda v: v.reshape(1, -1)

    def ffn(h, layer, idx, ln_idx):
        return _ffn_ln(h, ffn_w_gate, ffn_w_up, ffn_w_down, layer, idx,
                       row(ln_g[layer, ln_idx]), row(ln_b[layer, ln_idx]))

    for layer in range(DEPTH):
        h = ffn(h, layer, 0, 0)
        i = layer // 2
        g, b = row(ln_g[layer, 1]), row(ln_b[layer, 1])
        if layer % 2 == 0:
            y_pool, y_conv = _mixer_front(h, mix_w_in[i], pool_w[i], row(pool_scale[i]),
                                          conv_w[i], seq_len)
            h = _proj_ln(y_pool, y_conv, 0, 0, mix_w_out[i], h, g, b)
        else:
            qkv = _qkv_proj(h, attn_w_qkv[i])
            o = _attention(qkv, batch, seq_len, d // N_HEADS)
            h = _proj_ln(o, o, 0, 1, attn_w_out[i], h, g, b)
        h = ffn(h, layer, 1, 2)
    return h.reshape(batch, seq_len, d)
```

```python
import functools
import math

import jax
import jax.numpy as jnp
from jax import lax
from jax.experimental import pallas as pl
from jax.experimental.pallas import tpu as pltpu

F32 = jnp.float32
BF16 = jnp.bfloat16

DEPTH = 2
LN_EPS = 1e-5
DEEPNORM_ALPHA = (2.0 * DEPTH) ** 0.25
N_HEADS = 16
POOL_WINDOWS = (2, 4, 8, 16)
CONV_K = 3

V7X_VMEM_BYTES = 64 * 1024 * 1024
VMEM_LIMIT = V7X_VMEM_BYTES - 6 * 1024 * 1024

ROW_TILE = 1024
FF_TILE = 512
LN_ROWS = 256
PROJ_K_TILE = 512
PROJ_ROW_TILE = 1024
QKV_COL_TILE = 1024
ATTN_Q_TILE = 512
ATTN_K_TILE = 256
ATTN_HEADS = 2
ATTN_PAIR = 2
LOG2E = 1.4426950408889634
MASKED_LOGIT = -1e30
POOL_HALO = 32
CONV_HALO = 8


def _params(semantics):
    return pltpu.CompilerParams(dimension_semantics=semantics, vmem_limit_bytes=VMEM_LIMIT)


def _layer_norm_rows(y, g, b):
    mu = jnp.mean(y, axis=-1, keepdims=True)
    yc = y - mu
    var = jnp.mean(yc * yc, axis=-1, keepdims=True)
    return yc * lax.rsqrt(var + LN_EPS) * g + b


def _residual_init(x_hbm, o_ref, xb_ref, sem, branch_scale):
    tm = o_ref.shape[0]
    row0 = pl.program_id(0) * tm

    def chunk_copy(c):
        return pltpu.make_async_copy(x_hbm.at[pl.ds(row0 + c * LN_ROWS, LN_ROWS)],
                                     o_ref.at[pl.ds(c * LN_ROWS, LN_ROWS)], sem.at[c])

    chunks = range(tm // LN_ROWS)
    for c in chunks:
        chunk_copy(c).start()
    for c in chunks:
        chunk_copy(c).wait()
        rows = pl.ds(c * LN_ROWS, LN_ROWS)
        x = o_ref[rows, :]
        if xb_ref is not None:
            xb_ref[rows, :] = x.astype(BF16)
        o_ref[rows, :] = (DEEPNORM_ALPHA / branch_scale) * x


def _ln_epilogue(o_ref, g_ref, b_ref, branch_scale):
    g = g_ref[...]
    b = b_ref[...]
    for r in range(0, o_ref.shape[0], LN_ROWS):
        rows = pl.ds(r, LN_ROWS)
        o_ref[rows, :] = _layer_norm_rows(branch_scale * o_ref[rows, :], g, b)


def _ffn_kernel(x_hbm, wg_ref, wu_ref, wd_ref, g_ref, b_ref, o_ref, xb_ref, sem):
    j = pl.program_id(1)

    @pl.when(j == 0)
    def _():
        _residual_init(x_hbm, o_ref, xb_ref, sem, 0.5)

    xb = xb_ref[...]
    gate = jnp.dot(xb, wg_ref[...].astype(BF16), preferred_element_type=F32)
    up = jnp.dot(xb, wu_ref[...].astype(BF16), preferred_element_type=F32)
    h = (gate * jax.nn.sigmoid(gate) * up).astype(BF16)
    o_ref[...] += jnp.dot(h, wd_ref[...].astype(BF16), preferred_element_type=F32)

    @pl.when(j == pl.num_programs(1) - 1)
    def _():
        _ln_epilogue(o_ref, g_ref, b_ref, 0.5)


def _ffn_ln(x, wg, wu, wd, layer, idx, g, b):
    m, d = x.shape
    f = wg.shape[-1]
    grid = (m // ROW_TILE, f // FF_TILE)
    return pl.pallas_call(
        _ffn_kernel,
        name="ffn_ln",
        grid=grid,
        in_specs=[
            pl.BlockSpec(memory_space=pl.ANY),
            pl.BlockSpec((None, None, d, FF_TILE), lambda i, j: (layer, idx, 0, j)),
            pl.BlockSpec((None, None, d, FF_TILE), lambda i, j: (layer, idx, 0, j)),
            pl.BlockSpec((None, None, FF_TILE, d), lambda i, j: (layer, idx, j, 0)),
            pl.BlockSpec((1, d), lambda i, j: (0, 0)),
            pl.BlockSpec((1, d), lambda i, j: (0, 0)),
        ],
        out_specs=pl.BlockSpec((ROW_TILE, d), lambda i, j: (i, 0)),
        out_shape=jax.ShapeDtypeStruct((m, d), F32),
        scratch_shapes=[pltpu.VMEM((ROW_TILE, d), BF16),
                        pltpu.SemaphoreType.DMA((ROW_TILE // LN_ROWS,))],
        compiler_params=_params(("arbitrary", "arbitrary")),
    )(x, wg, wu, wd, g, b)


def _proj_kernel(a0_ref, a1_ref, w_ref, x_hbm, g_ref, b_ref, o_ref, sem, *, half_steps):
    k = pl.program_id(1)
    w = w_ref[...].astype(BF16)

    @pl.when(k == 0)
    def _():
        _residual_init(x_hbm, o_ref, None, sem, 1.0)

    @pl.when(k < half_steps)
    def _():
        o_ref[...] += jnp.dot(a0_ref[...], w, preferred_element_type=F32)

    @pl.when(k >= half_steps)
    def _():
        o_ref[...] += jnp.dot(a1_ref[...], w, preferred_element_type=F32)

    @pl.when(k == pl.num_programs(1) - 1)
    def _():
        _ln_epilogue(o_ref, g_ref, b_ref, 1.0)


def _proj_ln(a0, a1, col0, col1, w, x, g, b):
    m, d = x.shape
    kdim = w.shape[0]
    half = kdim // 2
    half_steps = half // PROJ_K_TILE
    grid = (m // PROJ_ROW_TILE, kdim // PROJ_K_TILE)
    a0_map = lambda i, k: (i, col0 * half_steps + jnp.minimum(k, half_steps - 1))
    a1_map = lambda i, k: (i, col1 * half_steps + jnp.maximum(k - half_steps, 0))
    return pl.pallas_call(
        functools.partial(_proj_kernel, half_steps=half_steps),
        name="proj_ln",
        grid=grid,
        in_specs=[
            pl.BlockSpec((PROJ_ROW_TILE, PROJ_K_TILE), a0_map),
            pl.BlockSpec((PROJ_ROW_TILE, PROJ_K_TILE), a1_map),
            pl.BlockSpec((PROJ_K_TILE, d), lambda i, k: (k, 0)),
            pl.BlockSpec(memory_space=pl.ANY),
            pl.BlockSpec((1, d), lambda i, k: (0, 0)),
            pl.BlockSpec((1, d), lambda i, k: (0, 0)),
        ],
        out_specs=pl.BlockSpec((PROJ_ROW_TILE, d), lambda i, k: (i, 0)),
        out_shape=jax.ShapeDtypeStruct((m, d), F32),
        scratch_shapes=[pltpu.SemaphoreType.DMA((PROJ_ROW_TILE // LN_ROWS,))],
        compiler_params=_params(("arbitrary", "arbitrary")),
    )(a0, a1, w, x, g, b)


def _mixer_kernel(x_ref, wp_ref, wb_ref, wc_ref, wx_ref, pw_ref, ps_ref, cw_ref,
                  yp_ref, yc_ref, xb_ref, pool_ref, conv_ref, pcarry_ref, ccarry_ref,
                  *, tiles_per_seq):
    i = pl.program_id(0)
    j = pl.program_id(1)
    tm = x_ref.shape[0]
    seq_tile = i % tiles_per_seq

    @pl.when(j == 0)
    def _():
        xb_ref[...] = x_ref[...].astype(BF16)

    @pl.when(seq_tile == 0)
    def _():
        pcarry_ref[j] = jnp.zeros(pcarry_ref.shape[1:], F32)
        ccarry_ref[j] = jnp.zeros(ccarry_ref.shape[1:], F32)

    xb = xb_ref[...]
    u = jnp.dot(xb, wp_ref[...].astype(BF16), preferred_element_type=F32)
    gate_b = jnp.dot(xb, wb_ref[...].astype(BF16), preferred_element_type=F32)
    gate_c = jnp.dot(xb, wc_ref[...].astype(BF16), preferred_element_type=F32)
    x_conv = jnp.dot(xb, wx_ref[...].astype(BF16), preferred_element_type=F32)

    base = 8 + POOL_HALO
    pool_ref[pl.ds(8, POOL_HALO), :] = pcarry_ref[j]
    pool_ref[pl.ds(base, tm), :] = u
    pcarry_ref[j] = u[tm - POOL_HALO:, :]
    for step in range(len(POOL_WINDOWS)):
        shift = 1 << step
        lo = 16 + 8 * step
        n = base + tm - lo

        @pl.when(j >= step)
        def _():
            pool_ref[pl.ds(lo, n), :] = (pool_ref[pl.ds(lo, n), :]
                                         + pool_ref[pl.ds(lo - shift, n), :])

    window = lax.shift_left(jnp.int32(2), j)
    pos = seq_tile * tm + lax.broadcasted_iota(jnp.int32, (tm, 1), 0)
    count = jnp.minimum(pos + 1, window).astype(F32)
    d = pool_ref[pl.ds(base, tm), :] / count - u
    y_pool = jnp.dot(d.astype(BF16), pw_ref[0].astype(BF16), preferred_element_type=F32)
    yp_ref[...] = (y_pool * ps_ref[...]).astype(yp_ref.dtype)

    z = gate_c * x_conv
    cbase = CONV_HALO
    conv_ref[pl.ds(0, CONV_HALO), :] = ccarry_ref[j]
    conv_ref[pl.ds(cbase, tm), :] = z
    ccarry_ref[j] = z[tm - CONV_HALO:, :]
    cw = cw_ref[...]
    y = z * cw[CONV_K - 1:CONV_K, :]
    for tap in range(CONV_K - 1):
        back = CONV_K - 1 - tap
        y = y + conv_ref[pl.ds(cbase - back, tm), :] * cw[tap:tap + 1, :]
    yc_ref[...] = (gate_b * y).astype(yc_ref.dtype)


def _mixer_front(x, w_in, pool_w, pool_scale, conv_w, seq_len):
    m, d = x.shape
    n_groups, group, _ = pool_w.shape
    pool_width = n_groups * group
    conv_width = conv_w.shape[1]
    assert conv_width == pool_width and w_in.shape[1] == pool_width + 3 * conv_width
    assert tuple(2 << g for g in range(n_groups)) == POOL_WINDOWS
    assert POOL_HALO >= POOL_WINDOWS[-1] - 1 and seq_len % ROW_TILE == 0
    chunk = group
    nb = pool_width // chunk
    grid = (m // ROW_TILE, n_groups)
    w_spec = lambda seg: pl.BlockSpec((d, chunk), lambda i, j: (0, seg * nb + j))
    out_spec = pl.BlockSpec((ROW_TILE, chunk), lambda i, j: (i, j))
    return pl.pallas_call(
        functools.partial(_mixer_kernel, tiles_per_seq=seq_len // ROW_TILE),
        name="mixer_front",
        grid=grid,
        in_specs=[
            pl.BlockSpec((ROW_TILE, d), lambda i, j: (i, 0)),
            w_spec(0), w_spec(1), w_spec(2), w_spec(3),
            pl.BlockSpec((1, group, group), lambda i, j: (j, 0, 0)),
            pl.BlockSpec((1, chunk), lambda i, j: (0, j)),
            pl.BlockSpec((CONV_K, chunk), lambda i, j: (0, j)),
        ],
        out_specs=[out_spec, out_spec],
        out_shape=[jax.ShapeDtypeStruct((m, pool_width), BF16),
                   jax.ShapeDtypeStruct((m, conv_width), BF16)],
        scratch_shapes=[
            pltpu.VMEM((ROW_TILE, d), BF16),
            pltpu.VMEM((8 + POOL_HALO + ROW_TILE, chunk), F32),
            pltpu.VMEM((CONV_HALO + ROW_TILE, chunk), F32),
            pltpu.VMEM((n_groups, POOL_HALO, chunk), F32),
            pltpu.VMEM((n_groups, CONV_HALO, chunk), F32),
        ],
        compiler_params=_params(("arbitrary", "arbitrary")),
    )(x, w_in, w_in, w_in, w_in, pool_w, pool_scale, conv_w)


def _qkv_kernel(x_ref, w_ref, o_ref, xb_ref, *, q_tiles, q_scale):
    j = pl.program_id(1)

    @pl.when(j == 0)
    def _():
        xb_ref[...] = x_ref[...].astype(BF16)

    y = jnp.dot(xb_ref[...], w_ref[...].astype(BF16), preferred_element_type=F32)
    col_scale = jnp.where(j < q_tiles, q_scale, 1.0).astype(F32)
    o_ref[...] = (y * col_scale).astype(o_ref.dtype)


def _qkv_proj(x, w, q_scale):
    m, d = x.shape
    n = w.shape[1]
    assert d % QKV_COL_TILE == 0
    grid = (m // ROW_TILE, n // QKV_COL_TILE)
    return pl.pallas_call(
        functools.partial(_qkv_kernel, q_tiles=d // QKV_COL_TILE, q_scale=q_scale),
        name="qkv_proj",
        grid=grid,
        in_specs=[
            pl.BlockSpec((ROW_TILE, d), lambda i, j: (i, 0)),
            pl.BlockSpec((d, QKV_COL_TILE), lambda i, j: (0, j)),
        ],
        out_specs=pl.BlockSpec((ROW_TILE, QKV_COL_TILE), lambda i, j: (i, j)),
        out_shape=jax.ShapeDtypeStruct((m, n), BF16),
        scratch_shapes=[pltpu.VMEM((ROW_TILE, d), BF16)],
        compiler_params=_params(("arbitrary", "arbitrary")),
    )(x, w)


def _attn_kernel(q_ref, k_ref, v_ref, tri_ref, o_ref, acc_ref, carry_ref, w_ref, hl_ref,
                 *, head_dim):
    seq = q_ref.shape[0]
    heads = q_ref.shape[1] // head_dim
    tq, tk = ATTN_Q_TILE, ATTN_K_TILE
    assert tq == ATTN_PAIR * tk
    contract_last = (((1,), (1,)), ((), ()))
    lanes = lambda g: slice(g * head_dim, (g + 1) * head_dim)

    pair = [(j, g) for j in range(ATTN_PAIR) for g in range(heads)]
    key_rows = lambda kb_first, j: pl.ds(pl.multiple_of((kb_first - j) * tk, tk), tk)

    def score_dots(q_start, kb_first):
        q = [q_ref[pl.ds(q_start, tq), lanes(g)] for g in range(heads)]
        return [lax.dot_general(q[g], k_ref[key_rows(kb_first, j), lanes(g)], contract_last,
                                preferred_element_type=F32) for j, g in pair]

    def store_scores(slot, w_all, q_start, kb_first, masked):
        for (j, g), w in zip(pair, w_all):
            sp = jnp.maximum(w, 0.0) + jnp.log(1.0 + jnp.exp2(-jnp.abs(w))) * LOG2E
            if masked:
                col = (kb_first - j) * tk + lax.broadcasted_iota(jnp.int32, (tq, tk), 1)
                row = q_start + lax.broadcasted_iota(jnp.int32, (tq, tk), 0)
                keep = col < row
                sp = jnp.where(keep, sp, 0.0)
                w = jnp.where(keep, w, MASKED_LOGIT)
            hi = sp.astype(BF16)
            w_ref[slot, j, g] = w
            hl_ref[slot, j, g, :, :tk] = hi
            hl_ref[slot, j, g, :, tk:] = (sp - hi.astype(F32)).astype(BF16)

    def suffix_dots(slot):
        return [jnp.dot(hl_ref[slot, j, g], tri_ref[...], preferred_element_type=F32)
                for j, g in pair]

    def accumulate(slot, suffix_all, kb_first):
        for (j, g), suffix in zip(pair, suffix_all):
            a = jnp.exp2(w_ref[slot, j, g] - suffix - carry_ref[g])
            acc_ref[g] += jnp.dot(a.astype(BF16), v_ref[key_rows(kb_first, j), lanes(g)],
                                  preferred_element_type=F32)
            carry_ref[g] += suffix[:, 0:1]

    def q_tile(qt, _):
        q_start = pl.multiple_of(qt * tq, tq)
        last_kb = ATTN_PAIR * qt + ATTN_PAIR - 1
        acc_ref[...] = jnp.zeros(acc_ref.shape, F32)
        carry_ref[...] = jnp.zeros(carry_ref.shape, F32)
        store_scores(0, score_dots(q_start, last_kb), q_start, last_kb, True)

        def off_diagonal_pair(p, _):
            slot = p % 2
            kb_first = last_kb - ATTN_PAIR * p
            suffix_all = suffix_dots(1 - slot)
            w_all = score_dots(q_start, kb_first)
            accumulate(1 - slot, suffix_all, kb_first + ATTN_PAIR)
            store_scores(slot, w_all, q_start, kb_first, False)
            return 0

        lax.fori_loop(1, qt + 1, off_diagonal_pair, 0)
        accumulate(qt % 2, suffix_dots(qt % 2), ATTN_PAIR - 1)
        for g in range(heads):
            o_ref[pl.ds(q_start, tq), lanes(g)] = acc_ref[g].astype(o_ref.dtype)
        return 0

    lax.fori_loop(0, seq // tq, q_tile, 0)


def _attention(qkv, batch, seq_len, head_dim):
    m = qkv.shape[0]
    tq, tk = ATTN_Q_TILE, ATTN_K_TILE
    assert seq_len % tq == 0 and N_HEADS % ATTN_HEADS == 0
    head_groups = N_HEADS // ATTN_HEADS
    width = ATTN_HEADS * head_dim
    tri = (lax.broadcasted_iota(jnp.int32, (tk, tk), 0)
           >= lax.broadcasted_iota(jnp.int32, (tk, tk), 1)).astype(BF16)
    tri2 = jnp.concatenate([tri, tri], axis=0)
    head_spec = lambda part: pl.BlockSpec((seq_len, width),
                                          lambda b, h: (b, part * head_groups + h))
    return pl.pallas_call(
        functools.partial(_attn_kernel, head_dim=head_dim),
        name="stickbreak_attn",
        grid=(batch, head_groups),
        in_specs=[head_spec(0), head_spec(1), head_spec(2),
                  pl.BlockSpec((2 * tk, tk), lambda b, h: (0, 0))],
        out_specs=pl.BlockSpec((seq_len, width), lambda b, h: (b, h)),
        out_shape=jax.ShapeDtypeStruct((m, N_HEADS * head_dim), BF16),
        scratch_shapes=[pltpu.VMEM((ATTN_HEADS, tq, head_dim), F32),
                        pltpu.VMEM((ATTN_HEADS, tq, 1), F32),
                        pltpu.VMEM((2, ATTN_PAIR, ATTN_HEADS, tq, tk), F32),
                        pltpu.VMEM((2, ATTN_PAIR, ATTN_HEADS, tq, 2 * tk), BF16)],
        compiler_params=_params(("arbitrary", "arbitrary")),
    )(qkv, qkv, qkv, tri2)


def kernel(x, ln_g, ln_b, ffn_w_gate, ffn_w_up, ffn_w_down, mix_w_in, pool_w, pool_scale,
           conv_w, mix_w_out, attn_w_qkv, attn_w_out):
    batch, seq_len, d = x.shape
    assert ln_g.shape[0] == DEPTH
    h = x.reshape(batch * seq_len, d)
    row = lambda v: v.reshape(1, -1)

    def ffn(h, layer, idx, ln_idx):
        return _ffn_ln(h, ffn_w_gate, ffn_w_up, ffn_w_down, layer, idx,
                       row(ln_g[layer, ln_idx]), row(ln_b[layer, ln_idx]))

    for layer in range(DEPTH):
        h = ffn(h, layer, 0, 0)
        i = layer // 2
        g, b = row(ln_g[layer, 1]), row(ln_b[layer, 1])
        if layer % 2 == 0:
            y_pool, y_conv = _mixer_front(h, mix_w_in[i], pool_w[i], row(pool_scale[i]),
                                          conv_w[i], seq_len)
            h = _proj_ln(y_pool, y_conv, 0, 0, mix_w_out[i], h, g, b)
        else:
            head_dim = d // N_HEADS
            qkv = _qkv_proj(h, attn_w_qkv[i], LOG2E / math.sqrt(head_dim))
            o = _attention(qkv, batch, seq_len, head_dim)
            h = _proj_ln(o, o, 0, 1, attn_w_out[i], h, g, b)
        h = ffn(h, layer, 1, 2)
    return h.reshape(batch, seq_len, d)
```

```python
import functools
import math

import jax
import jax.numpy as jnp
from jax import lax
from jax.experimental import pallas as pl
from jax.experimental.pallas import tpu as pltpu

F32 = jnp.float32
BF16 = jnp.bfloat16

DEPTH = 2
LN_EPS = 1e-5
DEEPNORM_ALPHA = (2.0 * DEPTH) ** 0.25
N_HEADS = 16
POOL_WINDOWS = (2, 4, 8, 16)
CONV_K = 3

V7X_VMEM_BYTES = 64 * 1024 * 1024
VMEM_LIMIT = V7X_VMEM_BYTES - 6 * 1024 * 1024

ROW_TILE = 1024
FF_TILE = 512
LN_ROWS = 256
PROJ_K_TILE = 512
PROJ_ROW_TILE = 1024
QKV_COL_TILE = 1024
ATTN_Q_TILE = 512
ATTN_K_TILE = 256
ATTN_HEADS = 2
ATTN_PAIR = 2
LOG2E = 1.4426950408889634
MASKED_LOGIT = -1e30
POOL_HALO = 32
CONV_HALO = 8


def _params(semantics):
    return pltpu.CompilerParams(dimension_semantics=semantics, vmem_limit_bytes=VMEM_LIMIT)


def _layer_norm_rows(y, g, b, eps):
    mu = jnp.mean(y, axis=-1, keepdims=True)
    yc = y - mu
    var = jnp.mean(yc * yc, axis=-1, keepdims=True)
    return yc * lax.rsqrt(var + eps) * g + b


def _residual_init(x_hbm, o_ref, xb_ref, sem, branch_scale):
    tm = o_ref.shape[0]
    row0 = pl.program_id(0) * tm

    def chunk_copy(c):
        return pltpu.make_async_copy(x_hbm.at[pl.ds(row0 + c * LN_ROWS, LN_ROWS)],
                                     o_ref.at[pl.ds(c * LN_ROWS, LN_ROWS)], sem.at[c])

    chunks = range(tm // LN_ROWS)
    for c in chunks:
        chunk_copy(c).start()
    for c in chunks:
        chunk_copy(c).wait()
        rows = pl.ds(c * LN_ROWS, LN_ROWS)
        x = o_ref[rows, :]
        if xb_ref is not None:
            xb_ref[rows, :] = x.astype(BF16)
        o_ref[rows, :] = (DEEPNORM_ALPHA / branch_scale) * x


def _ln_epilogue(o_ref, g_ref, b_ref, branch_scale):
    g = g_ref[...]
    b = b_ref[...]
    eps = LN_EPS / (branch_scale * branch_scale)
    for r in range(0, o_ref.shape[0], LN_ROWS):
        rows = pl.ds(r, LN_ROWS)
        o_ref[rows, :] = _layer_norm_rows(o_ref[rows, :], g, b, eps)


def _ffn_kernel(x_hbm, wg_ref, wu_ref, wd_ref, g_ref, b_ref, o_ref, xb_ref, sem):
    j = pl.program_id(1)

    @pl.when(j == 0)
    def _():
        _residual_init(x_hbm, o_ref, xb_ref, sem, 0.5)

    xb = xb_ref[...]
    gate = jnp.dot(xb, wg_ref[...].astype(BF16), preferred_element_type=F32)
    up = jnp.dot(xb, wu_ref[...].astype(BF16), preferred_element_type=F32)
    h = (gate * jax.nn.sigmoid(gate) * up).astype(BF16)
    o_ref[...] += jnp.dot(h, wd_ref[...].astype(BF16), preferred_element_type=F32)

    @pl.when(j == pl.num_programs(1) - 1)
    def _():
        _ln_epilogue(o_ref, g_ref, b_ref, 0.5)


def _ffn_ln(x, wg, wu, wd, layer, idx, g, b):
    m, d = x.shape
    f = wg.shape[-1]
    grid = (m // ROW_TILE, f // FF_TILE)
    return pl.pallas_call(
        _ffn_kernel,
        name="ffn_ln",
        grid=grid,
        in_specs=[
            pl.BlockSpec(memory_space=pl.ANY),
            pl.BlockSpec((None, None, d, FF_TILE), lambda i, j: (layer, idx, 0, j)),
            pl.BlockSpec((None, None, d, FF_TILE), lambda i, j: (layer, idx, 0, j)),
            pl.BlockSpec((None, None, FF_TILE, d), lambda i, j: (layer, idx, j, 0)),
            pl.BlockSpec((1, d), lambda i, j: (0, 0)),
            pl.BlockSpec((1, d), lambda i, j: (0, 0)),
        ],
        out_specs=pl.BlockSpec((ROW_TILE, d), lambda i, j: (i, 0)),
        out_shape=jax.ShapeDtypeStruct((m, d), F32),
        scratch_shapes=[pltpu.VMEM((ROW_TILE, d), BF16),
                        pltpu.SemaphoreType.DMA((ROW_TILE // LN_ROWS,))],
        compiler_params=_params(("arbitrary", "arbitrary")),
    )(x, wg, wu, wd, g, b)


def _proj_kernel(a0_ref, a1_ref, w_ref, x_hbm, g_ref, b_ref, o_ref, wb_ref, sem,
                 *, half_steps):
    k = pl.program_id(1)
    w_rows = pl.ds(pl.multiple_of(k * w_ref.shape[0], w_ref.shape[0]), w_ref.shape[0])

    @pl.when(pl.program_id(0) == 0)
    def _():
        wb_ref[w_rows, :] = w_ref[...].astype(BF16)

    w = wb_ref[w_rows, :]

    @pl.when(k == 0)
    def _():
        _residual_init(x_hbm, o_ref, None, sem, 1.0)

    @pl.when(k < half_steps)
    def _():
        o_ref[...] += jnp.dot(a0_ref[...], w, preferred_element_type=F32)

    @pl.when(k >= half_steps)
    def _():
        o_ref[...] += jnp.dot(a1_ref[...], w, preferred_element_type=F32)

    @pl.when(k == pl.num_programs(1) - 1)
    def _():
        _ln_epilogue(o_ref, g_ref, b_ref, 1.0)


def _proj_ln(a0, a1, col0, col1, w, x, g, b):
    m, d = x.shape
    kdim = w.shape[0]
    half = kdim // 2
    half_steps = half // PROJ_K_TILE
    k_steps = kdim // PROJ_K_TILE
    grid = (m // PROJ_ROW_TILE, k_steps)
    a0_map = lambda i, k: (i, col0 * half_steps + jnp.minimum(k, half_steps - 1))
    a1_map = lambda i, k: (i, col1 * half_steps + jnp.maximum(k - half_steps, 0))
    w_map = lambda i, k: (jnp.where(i == 0, k, k_steps - 1), 0)
    return pl.pallas_call(
        functools.partial(_proj_kernel, half_steps=half_steps),
        name="proj_ln",
        grid=grid,
        in_specs=[
            pl.BlockSpec((PROJ_ROW_TILE, PROJ_K_TILE), a0_map),
            pl.BlockSpec((PROJ_ROW_TILE, PROJ_K_TILE), a1_map),
            pl.BlockSpec((PROJ_K_TILE, d), w_map),
            pl.BlockSpec(memory_space=pl.ANY),
            pl.BlockSpec((1, d), lambda i, k: (0, 0)),
            pl.BlockSpec((1, d), lambda i, k: (0, 0)),
        ],
        out_specs=pl.BlockSpec((PROJ_ROW_TILE, d), lambda i, k: (i, 0)),
        out_shape=jax.ShapeDtypeStruct((m, d), F32),
        scratch_shapes=[pltpu.VMEM((kdim, d), BF16),
                        pltpu.SemaphoreType.DMA((PROJ_ROW_TILE // LN_ROWS,))],
        compiler_params=_params(("arbitrary", "arbitrary")),
    )(a0, a1, w, x, g, b)


def _mixer_kernel(x_ref, wp_ref, wb_ref, wc_ref, wx_ref, pw_ref, ps_ref, cw_ref,
                  yp_ref, yc_ref, xb_ref, pool_ref, conv_ref, pcarry_ref, ccarry_ref,
                  *, tiles_per_seq):
    i = pl.program_id(0)
    j = pl.program_id(1)
    tm = x_ref.shape[0]
    seq_tile = i % tiles_per_seq

    @pl.when(j == 0)
    def _():
        xb_ref[...] = x_ref[...].astype(BF16)

    @pl.when(seq_tile == 0)
    def _():
        pcarry_ref[j] = jnp.zeros(pcarry_ref.shape[1:], F32)
        ccarry_ref[j] = jnp.zeros(ccarry_ref.shape[1:], F32)

    xb = xb_ref[...]
    u = jnp.dot(xb, wp_ref[...].astype(BF16), preferred_element_type=F32)
    gate_b = jnp.dot(xb, wb_ref[...].astype(BF16), preferred_element_type=F32)
    gate_c = jnp.dot(xb, wc_ref[...].astype(BF16), preferred_element_type=F32)
    x_conv = jnp.dot(xb, wx_ref[...].astype(BF16), preferred_element_type=F32)

    base = 8 + POOL_HALO
    pool_ref[pl.ds(8, POOL_HALO), :] = pcarry_ref[j]
    pool_ref[pl.ds(base, tm), :] = u
    pcarry_ref[j] = u[tm - POOL_HALO:, :]
    for step in range(len(POOL_WINDOWS)):
        shift = 1 << step
        lo = 16 + 8 * step
        n = base + tm - lo

        @pl.when(j >= step)
        def _():
            pool_ref[pl.ds(lo, n), :] = (pool_ref[pl.ds(lo, n), :]
                                         + pool_ref[pl.ds(lo - shift, n), :])

    window = lax.shift_left(jnp.int32(2), j)
    pos = seq_tile * tm + lax.broadcasted_iota(jnp.int32, (tm, 1), 0)
    count = jnp.minimum(pos + 1, window).astype(F32)
    d = pool_ref[pl.ds(base, tm), :] / count - u
    y_pool = jnp.dot(d.astype(BF16), pw_ref[0].astype(BF16), preferred_element_type=F32)
    yp_ref[...] = (y_pool * ps_ref[...]).astype(yp_ref.dtype)

    z = gate_c * x_conv
    cbase = CONV_HALO
    conv_ref[pl.ds(0, CONV_HALO), :] = ccarry_ref[j]
    conv_ref[pl.ds(cbase, tm), :] = z
    ccarry_ref[j] = z[tm - CONV_HALO:, :]
    cw = cw_ref[...]
    y = z * cw[CONV_K - 1:CONV_K, :]
    for tap in range(CONV_K - 1):
        back = CONV_K - 1 - tap
        y = y + conv_ref[pl.ds(cbase - back, tm), :] * cw[tap:tap + 1, :]
    yc_ref[...] = (gate_b * y).astype(yc_ref.dtype)


def _mixer_front(x, w_in, pool_w, pool_scale, conv_w, seq_len):
    m, d = x.shape
    n_groups, group, _ = pool_w.shape
    pool_width = n_groups * group
    conv_width = conv_w.shape[1]
    assert conv_width == pool_width and w_in.shape[1] == pool_width + 3 * conv_width
    assert tuple(2 << g for g in range(n_groups)) == POOL_WINDOWS
    assert POOL_HALO >= POOL_WINDOWS[-1] - 1 and seq_len % ROW_TILE == 0
    chunk = group
    nb = pool_width // chunk
    grid = (m // ROW_TILE, n_groups)
    w_spec = lambda seg: pl.BlockSpec((d, chunk), lambda i, j: (0, seg * nb + j))
    out_spec = pl.BlockSpec((ROW_TILE, chunk), lambda i, j: (i, j))
    return pl.pallas_call(
        functools.partial(_mixer_kernel, tiles_per_seq=seq_len // ROW_TILE),
        name="mixer_front",
        grid=grid,
        in_specs=[
            pl.BlockSpec((ROW_TILE, d), lambda i, j: (i, 0)),
            w_spec(0), w_spec(1), w_spec(2), w_spec(3),
            pl.BlockSpec((1, group, group), lambda i, j: (j, 0, 0)),
            pl.BlockSpec((1, chunk), lambda i, j: (0, j)),
            pl.BlockSpec((CONV_K, chunk), lambda i, j: (0, j)),
        ],
        out_specs=[out_spec, out_spec],
        out_shape=[jax.ShapeDtypeStruct((m, pool_width), BF16),
                   jax.ShapeDtypeStruct((m, conv_width), BF16)],
        scratch_shapes=[
            pltpu.VMEM((ROW_TILE, d), BF16),
            pltpu.VMEM((8 + POOL_HALO + ROW_TILE, chunk), F32),
            pltpu.VMEM((CONV_HALO + ROW_TILE, chunk), F32),
            pltpu.VMEM((n_groups, POOL_HALO, chunk), F32),
            pltpu.VMEM((n_groups, CONV_HALO, chunk), F32),
        ],
        compiler_params=_params(("arbitrary", "arbitrary")),
    )(x, w_in, w_in, w_in, w_in, pool_w, pool_scale, conv_w)


def _qkv_kernel(x_ref, w_ref, o_ref, xb_ref, *, q_tiles, q_scale):
    j = pl.program_id(1)

    @pl.when(j == 0)
    def _():
        xb_ref[...] = x_ref[...].astype(BF16)

    y = jnp.dot(xb_ref[...], w_ref[...].astype(BF16), preferred_element_type=F32)
    col_scale = jnp.where(j < q_tiles, q_scale, 1.0).astype(F32)
    o_ref[...] = (y * col_scale).astype(o_ref.dtype)


def _qkv_proj(x, w, q_scale):
    m, d = x.shape
    n = w.shape[1]
    assert d % QKV_COL_TILE == 0
    grid = (m // ROW_TILE, n // QKV_COL_TILE)
    return pl.pallas_call(
        functools.partial(_qkv_kernel, q_tiles=d // QKV_COL_TILE, q_scale=q_scale),
        name="qkv_proj",
        grid=grid,
        in_specs=[
            pl.BlockSpec((ROW_TILE, d), lambda i, j: (i, 0)),
            pl.BlockSpec((d, QKV_COL_TILE), lambda i, j: (0, j)),
        ],
        out_specs=pl.BlockSpec((ROW_TILE, QKV_COL_TILE), lambda i, j: (i, j)),
        out_shape=jax.ShapeDtypeStruct((m, n), BF16),
        scratch_shapes=[pltpu.VMEM((ROW_TILE, d), BF16)],
        compiler_params=_params(("arbitrary", "arbitrary")),
    )(x, w)


def _attn_kernel(q_ref, k_ref, v_ref, tri_ref, o_ref, acc_ref, carry_ref,
                 w0_ref, w1_ref, sp0_ref, sp1_ref, *, head_dim):
    seq = q_ref.shape[0]
    heads = q_ref.shape[1] // head_dim
    tq, tk = ATTN_Q_TILE, ATTN_K_TILE
    assert tq == ATTN_PAIR * tk
    contract_last = (((1,), (1,)), ((), ()))
    lanes = lambda g: slice(g * head_dim, (g + 1) * head_dim)
    w_slots = (w0_ref, w1_ref)
    sp_slots = (sp0_ref, sp1_ref)

    pair = [(j, g) for j in range(ATTN_PAIR) for g in range(heads)]
    key_rows = lambda kb_first, j: pl.ds(pl.multiple_of((kb_first - j) * tk, tk), tk)

    def score_dot(slot, j, g, q_start, kb_first):
        q = q_ref[pl.ds(q_start, tq), lanes(g)]
        w_slots[slot][j, g] = lax.dot_general(
            q, k_ref[key_rows(kb_first, j), lanes(g)], contract_last,
            preferred_element_type=F32)

    def softplus_one(slot, j, g, q_start, kb_first, masked):
        w_ref = w_slots[slot]
        w = w_ref[j, g]
        sp = jnp.maximum(w, 0.0) + jnp.log(1.0 + jnp.exp2(-jnp.abs(w))) * LOG2E
        if masked:
            col = (kb_first - j) * tk + lax.broadcasted_iota(jnp.int32, (tq, tk), 1)
            row = q_start + lax.broadcasted_iota(jnp.int32, (tq, tk), 0)
            keep = col < row
            sp = jnp.where(keep, sp, 0.0)
            w_ref[j, g] = jnp.where(keep, w, MASKED_LOGIT)
        sp_slots[slot][j, g] = sp.astype(BF16)

    def suffix_dot(slot, j, g):
        return jnp.dot(sp_slots[slot][j, g], tri_ref[...], preferred_element_type=F32)

    def accumulate_one(slot, j, g, suffix, kb_first):
        a = jnp.exp2(w_slots[slot][j, g] - suffix - carry_ref[g])
        acc_ref[g] += jnp.dot(a.astype(BF16), v_ref[key_rows(kb_first, j), lanes(g)],
                              preferred_element_type=F32)
        carry_ref[g] += suffix[:, 0:1]

    def first_pair(q_start, kb_first):
        for j, g in pair:
            score_dot(0, j, g, q_start, kb_first)
        for j, g in pair:
            softplus_one(0, j, g, q_start, kb_first, True)

    def last_pair(slot, kb_first):
        suffix_all = [suffix_dot(slot, j, g) for j, g in pair]
        for (j, g), suffix in zip(pair, suffix_all):
            accumulate_one(slot, j, g, suffix, kb_first)

    def pair_step(slot, q_start, kb_first):
        prev = 1 - slot
        pending = None
        for j, g in pair:
            suffix = suffix_dot(prev, j, g)
            score_dot(slot, j, g, q_start, kb_first)
            if pending is not None:
                accumulate_one(prev, *pending, kb_first + ATTN_PAIR)
            pending = (j, g, suffix)
            softplus_one(slot, j, g, q_start, kb_first, False)
        accumulate_one(prev, *pending, kb_first + ATTN_PAIR)

    def q_tile(qt, _):
        q_start = pl.multiple_of(qt * tq, tq)
        last_kb = ATTN_PAIR * qt + ATTN_PAIR - 1
        acc_ref[...] = jnp.zeros(acc_ref.shape, F32)
        carry_ref[...] = jnp.zeros(carry_ref.shape, F32)
        first_pair(q_start, last_kb)

        def off_diagonal_pair(p, _):
            for slot in range(2):
                @pl.when(p % 2 == slot)
                def _():
                    pair_step(slot, q_start, last_kb - ATTN_PAIR * p)
            return 0

        lax.fori_loop(1, qt + 1, off_diagonal_pair, 0)
        for slot in range(2):
            @pl.when(qt % 2 == slot)
            def _():
                last_pair(slot, ATTN_PAIR - 1)
        for g in range(heads):
            o_ref[pl.ds(q_start, tq), lanes(g)] = acc_ref[g].astype(o_ref.dtype)
        return 0

    lax.fori_loop(0, seq // tq, q_tile, 0)


def _attention(qkv, batch, seq_len, head_dim):
    m = qkv.shape[0]
    tq, tk = ATTN_Q_TILE, ATTN_K_TILE
    assert seq_len % tq == 0 and N_HEADS % ATTN_HEADS == 0
    head_groups = N_HEADS // ATTN_HEADS
    width = ATTN_HEADS * head_dim
    tri = (lax.broadcasted_iota(jnp.int32, (tk, tk), 0)
           >= lax.broadcasted_iota(jnp.int32, (tk, tk), 1)).astype(BF16)
    head_spec = lambda part: pl.BlockSpec((seq_len, width),
                                          lambda b, h: (b, part * head_groups + h))
    w_slot = pltpu.VMEM((ATTN_PAIR, ATTN_HEADS, tq, tk), F32)
    sp_slot = pltpu.VMEM((ATTN_PAIR, ATTN_HEADS, tq, tk), BF16)
    return pl.pallas_call(
        functools.partial(_attn_kernel, head_dim=head_dim),
        name="stickbreak_attn",
        grid=(batch, head_groups),
        in_specs=[head_spec(0), head_spec(1), head_spec(2),
                  pl.BlockSpec((tk, tk), lambda b, h: (0, 0))],
        out_specs=pl.BlockSpec((seq_len, width), lambda b, h: (b, h)),
        out_shape=jax.ShapeDtypeStruct((m, N_HEADS * head_dim), BF16),
        scratch_shapes=[pltpu.VMEM((ATTN_HEADS, tq, head_dim), F32),
                        pltpu.VMEM((ATTN_HEADS, tq, 1), F32),
                        w_slot, w_slot, sp_slot, sp_slot],
        compiler_params=_params(("arbitrary", "arbitrary")),
    )(qkv, qkv, qkv, tri)


def kernel(x, ln_g, ln_b, ffn_w_gate, ffn_w_up, ffn_w_down, mix_w_in, pool_w, pool_scale,
           conv_w, mix_w_out, attn_w_qkv, attn_w_out):
    batch, seq_len, d = x.shape
    assert ln_g.shape[0] == DEPTH
    h = x.reshape(batch * seq_len, d)
    row = lambda v: v.reshape(1, -1)

    def ffn(h, layer, idx, ln_idx):
        return _ffn_ln(h, ffn_w_gate, ffn_w_up, ffn_w_down, layer, idx,
                       row(ln_g[layer, ln_idx]), row(ln_b[layer, ln_idx]))

    for layer in range(DEPTH):
        h = ffn(h, layer, 0, 0)
        i = layer // 2
        g, b = row(ln_g[layer, 1]), row(ln_b[layer, 1])
        if layer % 2 == 0:
            y_pool, y_conv = _mixer_front(h, mix_w_in[i], pool_w[i], row(pool_scale[i]),
                                          conv_w[i], seq_len)
            h = _proj_ln(y_pool, y_conv, 0, 0, mix_w_out[i], h, g, b)
        else:
            head_dim = d // N_HEADS
            qkv = _qkv_proj(h, attn_w_qkv[i], LOG2E / math.sqrt(head_dim))
            o = _attention(qkv, batch, seq_len, head_dim)
            h = _proj_ln(o, o, 0, 1, attn_w_out[i], h, g, b)
        h = ffn(h, layer, 1, 2)
    return h.reshape(batch, seq_len, d)
```

```python
import functools
import math

import jax
import jax.numpy as jnp
from jax import lax
from jax.experimental import pallas as pl
from jax.experimental.pallas import tpu as pltpu

F32 = jnp.float32
BF16 = jnp.bfloat16

DEPTH = 2
LN_EPS = 1e-5
DEEPNORM_ALPHA = (2.0 * DEPTH) ** 0.25
N_HEADS = 16
POOL_WINDOWS = (2, 4, 8, 16)
CONV_K = 3

V7X_VMEM_BYTES = 64 * 1024 * 1024
VMEM_LIMIT = V7X_VMEM_BYTES - 6 * 1024 * 1024

ROW_TILE = 1024
FF_TILE = 512
LN_ROWS = 256
PROJ_K_TILE = 512
PROJ_ROW_TILE = 1024
QKV_COL_TILE = 1024
ATTN_Q_TILE = 512
ATTN_K_TILE = 256
ATTN_HEADS = 2
ATTN_PAIR = 2
LOG2E = 1.4426950408889634
MASKED_LOGIT = -1e30
POOL_HALO = 32
CONV_HALO = 8


def _params(semantics):
    return pltpu.CompilerParams(dimension_semantics=semantics, vmem_limit_bytes=VMEM_LIMIT)


def _layer_norm_rows(y, g, b, eps):
    mu = jnp.mean(y, axis=-1, keepdims=True)
    yc = y - mu
    var = jnp.mean(yc * yc, axis=-1, keepdims=True)
    return yc * lax.rsqrt(var + eps) * g + b


class _ResidualTiles:
    def __init__(self, x_hbm, o_hbm, acc_ref, x_sem, o_sem, branch_scale):
        self.x_hbm, self.o_hbm, self.acc_ref = x_hbm, o_hbm, acc_ref
        self.x_sem, self.o_sem = x_sem, o_sem
        self.branch_scale = branch_scale
        self.rows = acc_ref.shape[1]
        self.tile = pl.program_id(0)
        self.slot = self.tile % 2
        self.acc = acc_ref.at[self.slot]

    def _x_copy(self, tile, slot):
        return pltpu.make_async_copy(self.x_hbm.at[pl.ds(tile * self.rows, self.rows)],
                                     self.acc_ref.at[slot], self.x_sem.at[slot])

    def _o_copy(self, tile, slot):
        return pltpu.make_async_copy(self.acc_ref.at[slot],
                                     self.o_hbm.at[pl.ds(tile * self.rows, self.rows)],
                                     self.o_sem.at[slot])

    def begin_step(self, step, xb_ref):
        tile, slot, other = self.tile, self.slot, 1 - self.slot

        @pl.when(jnp.logical_and(tile == 0, step == 0))
        def _():
            self._x_copy(0, 0).start()

        @pl.when(step == 0)
        def _():
            self._x_copy(tile, slot).wait()
            for r in range(0, self.rows, LN_ROWS):
                rows = pl.ds(r, LN_ROWS)
                x = self.acc[rows, :]
                if xb_ref is not None:
                    xb_ref[rows, :] = x.astype(BF16)
                self.acc[rows, :] = (DEEPNORM_ALPHA / self.branch_scale) * x

        @pl.when(step == 1)
        def _():
            @pl.when(tile >= 1)
            def _():
                self._o_copy(tile - 1, other).wait()

            @pl.when(tile + 1 < pl.num_programs(0))
            def _():
                self._x_copy(tile + 1, other).start()

    def finish_tile(self, g_ref, b_ref):
        g = g_ref[...]
        b = b_ref[...]
        eps = LN_EPS / (self.branch_scale * self.branch_scale)
        for r in range(0, self.rows, LN_ROWS):
            rows = pl.ds(r, LN_ROWS)
            self.acc[rows, :] = _layer_norm_rows(self.acc[rows, :], g, b, eps)
        self._o_copy(self.tile, self.slot).start()

        @pl.when(self.tile == pl.num_programs(0) - 1)
        def _():
            self._o_copy(self.tile, self.slot).wait()


def _ffn_kernel(x_hbm, wg_ref, wu_ref, wd_ref, g_ref, b_ref, o_hbm,
                acc_ref, xb_ref, x_sem, o_sem):
    j = pl.program_id(1)
    tiles = _ResidualTiles(x_hbm, o_hbm, acc_ref, x_sem, o_sem, 0.5)
    tiles.begin_step(j, xb_ref)

    xb = xb_ref[...]
    gate = jnp.dot(xb, wg_ref[...].astype(BF16), preferred_element_type=F32)
    up = jnp.dot(xb, wu_ref[...].astype(BF16), preferred_element_type=F32)
    h = (gate * jax.nn.sigmoid(gate) * up).astype(BF16)
    tiles.acc[...] += jnp.dot(h, wd_ref[...].astype(BF16), preferred_element_type=F32)

    @pl.when(j == pl.num_programs(1) - 1)
    def _():
        tiles.finish_tile(g_ref, b_ref)


def _ffn_ln(x, wg, wu, wd, layer, idx, g, b):
    m, d = x.shape
    f = wg.shape[-1]
    grid = (m // ROW_TILE, f // FF_TILE)
    return pl.pallas_call(
        _ffn_kernel,
        name="ffn_ln",
        grid=grid,
        in_specs=[
            pl.BlockSpec(memory_space=pl.ANY),
            pl.BlockSpec((None, None, d, FF_TILE), lambda i, j: (layer, idx, 0, j)),
            pl.BlockSpec((None, None, d, FF_TILE), lambda i, j: (layer, idx, 0, j)),
            pl.BlockSpec((None, None, FF_TILE, d), lambda i, j: (layer, idx, j, 0)),
            pl.BlockSpec((1, d), lambda i, j: (0, 0)),
            pl.BlockSpec((1, d), lambda i, j: (0, 0)),
        ],
        out_specs=pl.BlockSpec(memory_space=pl.ANY),
        out_shape=jax.ShapeDtypeStruct((m, d), F32),
        scratch_shapes=[pltpu.VMEM((2, ROW_TILE, d), F32),
                        pltpu.VMEM((ROW_TILE, d), BF16),
                        pltpu.SemaphoreType.DMA((2,)),
                        pltpu.SemaphoreType.DMA((2,))],
        compiler_params=_params(("arbitrary", "arbitrary")),
    )(x, wg, wu, wd, g, b)


def _proj_kernel(a0_ref, a1_ref, w_ref, x_hbm, g_ref, b_ref, o_hbm,
                 acc_ref, wb_ref, x_sem, o_sem, *, half_steps):
    k = pl.program_id(1)
    tiles = _ResidualTiles(x_hbm, o_hbm, acc_ref, x_sem, o_sem, 1.0)
    tiles.begin_step(k, None)
    w_rows = pl.ds(pl.multiple_of(k * w_ref.shape[0], w_ref.shape[0]), w_ref.shape[0])

    @pl.when(pl.program_id(0) == 0)
    def _():
        wb_ref[w_rows, :] = w_ref[...].astype(BF16)

    w = wb_ref[w_rows, :]

    @pl.when(k < half_steps)
    def _():
        tiles.acc[...] += jnp.dot(a0_ref[...], w, preferred_element_type=F32)

    @pl.when(k >= half_steps)
    def _():
        tiles.acc[...] += jnp.dot(a1_ref[...], w, preferred_element_type=F32)

    @pl.when(k == pl.num_programs(1) - 1)
    def _():
        tiles.finish_tile(g_ref, b_ref)


def _proj_ln(a0, a1, col0, col1, w, x, g, b):
    m, d = x.shape
    kdim = w.shape[0]
    half = kdim // 2
    half_steps = half // PROJ_K_TILE
    k_steps = kdim // PROJ_K_TILE
    grid = (m // PROJ_ROW_TILE, k_steps)
    a0_map = lambda i, k: (i, col0 * half_steps + jnp.minimum(k, half_steps - 1))
    a1_map = lambda i, k: (i, col1 * half_steps + jnp.maximum(k - half_steps, 0))
    w_map = lambda i, k: (jnp.where(i == 0, k, k_steps - 1), 0)
    return pl.pallas_call(
        functools.partial(_proj_kernel, half_steps=half_steps),
        name="proj_ln",
        grid=grid,
        in_specs=[
            pl.BlockSpec((PROJ_ROW_TILE, PROJ_K_TILE), a0_map),
            pl.BlockSpec((PROJ_ROW_TILE, PROJ_K_TILE), a1_map),
            pl.BlockSpec((PROJ_K_TILE, d), w_map),
            pl.BlockSpec(memory_space=pl.ANY),
            pl.BlockSpec((1, d), lambda i, k: (0, 0)),
            pl.BlockSpec((1, d), lambda i, k: (0, 0)),
        ],
        out_specs=pl.BlockSpec(memory_space=pl.ANY),
        out_shape=jax.ShapeDtypeStruct((m, d), F32),
        scratch_shapes=[pltpu.VMEM((2, PROJ_ROW_TILE, d), F32),
                        pltpu.VMEM((kdim, d), BF16),
                        pltpu.SemaphoreType.DMA((2,)),
                        pltpu.SemaphoreType.DMA((2,))],
        compiler_params=_params(("arbitrary", "arbitrary")),
    )(a0, a1, w, x, g, b)


def _mixer_kernel(x_ref, wp_ref, wb_ref, wc_ref, wx_ref, pw_ref, ps_ref, cw_ref,
                  yp_ref, yc_ref, xb_ref, pool_ref, conv_ref, pcarry_ref, ccarry_ref,
                  *, tiles_per_seq):
    i = pl.program_id(0)
    j = pl.program_id(1)
    tm = x_ref.shape[0]
    seq_tile = i % tiles_per_seq

    @pl.when(j == 0)
    def _():
        xb_ref[...] = x_ref[...].astype(BF16)

    @pl.when(seq_tile == 0)
    def _():
        pcarry_ref[j] = jnp.zeros(pcarry_ref.shape[1:], F32)
        ccarry_ref[j] = jnp.zeros(ccarry_ref.shape[1:], F32)

    xb = xb_ref[...]
    u = jnp.dot(xb, wp_ref[...].astype(BF16), preferred_element_type=F32)
    gate_b = jnp.dot(xb, wb_ref[...].astype(BF16), preferred_element_type=F32)
    gate_c = jnp.dot(xb, wc_ref[...].astype(BF16), preferred_element_type=F32)
    x_conv = jnp.dot(xb, wx_ref[...].astype(BF16), preferred_element_type=F32)

    base = 8 + POOL_HALO
    pool_ref[pl.ds(8, POOL_HALO), :] = pcarry_ref[j]
    pool_ref[pl.ds(base, tm), :] = u
    pcarry_ref[j] = u[tm - POOL_HALO:, :]
    for step in range(len(POOL_WINDOWS)):
        shift = 1 << step
        lo = 16 + 8 * step
        n = base + tm - lo

        @pl.when(j >= step)
        def _():
            pool_ref[pl.ds(lo, n), :] = (pool_ref[pl.ds(lo, n), :]
                                         + pool_ref[pl.ds(lo - shift, n), :])

    window = lax.shift_left(jnp.int32(2), j)
    pos = seq_tile * tm + lax.broadcasted_iota(jnp.int32, (tm, 1), 0)
    count = jnp.minimum(pos + 1, window).astype(F32)
    d = pool_ref[pl.ds(base, tm), :] / count - u
    y_pool = jnp.dot(d.astype(BF16), pw_ref[0].astype(BF16), preferred_element_type=F32)
    yp_ref[...] = (y_pool * ps_ref[...]).astype(yp_ref.dtype)

    z = gate_c * x_conv
    cbase = CONV_HALO
    conv_ref[pl.ds(0, CONV_HALO), :] = ccarry_ref[j]
    conv_ref[pl.ds(cbase, tm), :] = z
    ccarry_ref[j] = z[tm - CONV_HALO:, :]
    cw = cw_ref[...]
    y = z * cw[CONV_K - 1:CONV_K, :]
    for tap in range(CONV_K - 1):
        back = CONV_K - 1 - tap
        y = y + conv_ref[pl.ds(cbase - back, tm), :] * cw[tap:tap + 1, :]
    yc_ref[...] = (gate_b * y).astype(yc_ref.dtype)


def _mixer_front(x, w_in, pool_w, pool_scale, conv_w, seq_len):
    m, d = x.shape
    n_groups, group, _ = pool_w.shape
    pool_width = n_groups * group
    conv_width = conv_w.shape[1]
    assert conv_width == pool_width and w_in.shape[1] == pool_width + 3 * conv_width
    assert tuple(2 << g for g in range(n_groups)) == POOL_WINDOWS
    assert POOL_HALO >= POOL_WINDOWS[-1] - 1 and seq_len % ROW_TILE == 0
    chunk = group
    nb = pool_width // chunk
    grid = (m // ROW_TILE, n_groups)
    w_spec = lambda seg: pl.BlockSpec((d, chunk), lambda i, j: (0, seg * nb + j))
    out_spec = pl.BlockSpec((ROW_TILE, chunk), lambda i, j: (i, j))
    return pl.pallas_call(
        functools.partial(_mixer_kernel, tiles_per_seq=seq_len // ROW_TILE),
        name="mixer_front",
        grid=grid,
        in_specs=[
            pl.BlockSpec((ROW_TILE, d), lambda i, j: (i, 0)),
            w_spec(0), w_spec(1), w_spec(2), w_spec(3),
            pl.BlockSpec((1, group, group), lambda i, j: (j, 0, 0)),
            pl.BlockSpec((1, chunk), lambda i, j: (0, j)),
            pl.BlockSpec((CONV_K, chunk), lambda i, j: (0, j)),
        ],
        out_specs=[out_spec, out_spec],
        out_shape=[jax.ShapeDtypeStruct((m, pool_width), BF16),
                   jax.ShapeDtypeStruct((m, conv_width), BF16)],
        scratch_shapes=[
            pltpu.VMEM((ROW_TILE, d), BF16),
            pltpu.VMEM((8 + POOL_HALO + ROW_TILE, chunk), F32),
            pltpu.VMEM((CONV_HALO + ROW_TILE, chunk), F32),
            pltpu.VMEM((n_groups, POOL_HALO, chunk), F32),
            pltpu.VMEM((n_groups, CONV_HALO, chunk), F32),
        ],
        compiler_params=_params(("arbitrary", "arbitrary")),
    )(x, w_in, w_in, w_in, w_in, pool_w, pool_scale, conv_w)


def _qkv_kernel(x_ref, w_ref, o_ref, xb_ref, *, q_tiles, q_scale):
    j = pl.program_id(1)

    @pl.when(j == 0)
    def _():
        xb_ref[...] = x_ref[...].astype(BF16)

    y = jnp.dot(xb_ref[...], w_ref[...].astype(BF16), preferred_element_type=F32)
    col_scale = jnp.where(j < q_tiles, q_scale, 1.0).astype(F32)
    o_ref[...] = (y * col_scale).astype(o_ref.dtype)


def _qkv_proj(x, w, q_scale):
    m, d = x.shape
    n = w.shape[1]
    assert d % QKV_COL_TILE == 0
    grid = (m // ROW_TILE, n // QKV_COL_TILE)
    return pl.pallas_call(
        functools.partial(_qkv_kernel, q_tiles=d // QKV_COL_TILE, q_scale=q_scale),
        name="qkv_proj",
        grid=grid,
        in_specs=[
            pl.BlockSpec((ROW_TILE, d), lambda i, j: (i, 0)),
            pl.BlockSpec((d, QKV_COL_TILE), lambda i, j: (0, j)),
        ],
        out_specs=pl.BlockSpec((ROW_TILE, QKV_COL_TILE), lambda i, j: (i, j)),
        out_shape=jax.ShapeDtypeStruct((m, n), BF16),
        scratch_shapes=[pltpu.VMEM((ROW_TILE, d), BF16)],
        compiler_params=_params(("arbitrary", "arbitrary")),
    )(x, w)


def _attn_kernel(q_ref, k_ref, v_ref, tri_ref, o_ref, acc_ref, carry_ref,
                 w0_ref, w1_ref, sp0_ref, sp1_ref, *, head_dim):
    seq = q_ref.shape[0]
    heads = q_ref.shape[1] // head_dim
    tq, tk = ATTN_Q_TILE, ATTN_K_TILE
    assert tq == ATTN_PAIR * tk
    contract_last = (((1,), (1,)), ((), ()))
    lanes = lambda g: slice(g * head_dim, (g + 1) * head_dim)
    w_slots = (w0_ref, w1_ref)
    sp_slots = (sp0_ref, sp1_ref)

    pair = [(j, g) for j in range(ATTN_PAIR) for g in range(heads)]
    key_rows = lambda kb_first, j: pl.ds(pl.multiple_of((kb_first - j) * tk, tk), tk)

    def score_dot(slot, j, g, q_start, kb_first):
        q = q_ref[pl.ds(q_start, tq), lanes(g)]
        w_slots[slot][j, g] = lax.dot_general(
            q, k_ref[key_rows(kb_first, j), lanes(g)], contract_last,
            preferred_element_type=F32)

    def softplus_one(slot, j, g, q_start, kb_first, masked):
        w_ref = w_slots[slot]
        w = w_ref[j, g]
        sp = jnp.maximum(w, 0.0) + jnp.log(1.0 + jnp.exp2(-jnp.abs(w))) * LOG2E
        if masked:
            col = (kb_first - j) * tk + lax.broadcasted_iota(jnp.int32, (tq, tk), 1)
            row = q_start + lax.broadcasted_iota(jnp.int32, (tq, tk), 0)
            keep = col < row
            sp = jnp.where(keep, sp, 0.0)
            w_ref[j, g] = jnp.where(keep, w, MASKED_LOGIT)
        sp_slots[slot][j, g] = sp.astype(BF16)

    def suffix_dot(slot, j, g):
        return jnp.dot(sp_slots[slot][j, g], tri_ref[...], preferred_element_type=F32)

    def accumulate_one(slot, j, g, suffix, kb_first):
        a = jnp.exp2(w_slots[slot][j, g] - suffix - carry_ref[g])
        acc_ref[g] += jnp.dot(a.astype(BF16), v_ref[key_rows(kb_first, j), lanes(g)],
                              preferred_element_type=F32)
        carry_ref[g] += suffix[:, 0:1]

    def first_pair(q_start, kb_first):
        for j, g in pair:
            score_dot(0, j, g, q_start, kb_first)
        for j, g in pair:
            softplus_one(0, j, g, q_start, kb_first, True)

    def last_pair(slot, kb_first):
        suffix_all = [suffix_dot(slot, j, g) for j, g in pair]
        for (j, g), suffix in zip(pair, suffix_all):
            accumulate_one(slot, j, g, suffix, kb_first)

    def pair_step(slot, q_start, kb_first):
        prev = 1 - slot
        pending = None
        for j, g in pair:
            suffix = suffix_dot(prev, j, g)
            score_dot(slot, j, g, q_start, kb_first)
            if pending is not None:
                accumulate_one(prev, *pending, kb_first + ATTN_PAIR)
            pending = (j, g, suffix)
            softplus_one(slot, j, g, q_start, kb_first, False)
        accumulate_one(prev, *pending, kb_first + ATTN_PAIR)

    def q_tile(qt, _):
        q_start = pl.multiple_of(qt * tq, tq)
        last_kb = ATTN_PAIR * qt + ATTN_PAIR - 1
        acc_ref[...] = jnp.zeros(acc_ref.shape, F32)
        carry_ref[...] = jnp.zeros(carry_ref.shape, F32)
        first_pair(q_start, last_kb)

        def off_diagonal_pair(p, _):
            for slot in range(2):
                @pl.when(p % 2 == slot)
                def _():
                    pair_step(slot, q_start, last_kb - ATTN_PAIR * p)
            return 0

        lax.fori_loop(1, qt + 1, off_diagonal_pair, 0)
        for slot in range(2):
            @pl.when(qt % 2 == slot)
            def _():
                last_pair(slot, ATTN_PAIR - 1)
        for g in range(heads):
            o_ref[pl.ds(q_start, tq), lanes(g)] = acc_ref[g].astype(o_ref.dtype)
        return 0

    lax.fori_loop(0, seq // tq, q_tile, 0)


def _attention(qkv, batch, seq_len, head_dim):
    m = qkv.shape[0]
    tq, tk = ATTN_Q_TILE, ATTN_K_TILE
    assert seq_len % tq == 0 and N_HEADS % ATTN_HEADS == 0
    head_groups = N_HEADS // ATTN_HEADS
    width = ATTN_HEADS * head_dim
    tri = (lax.broadcasted_iota(jnp.int32, (tk, tk), 0)
           >= lax.broadcasted_iota(jnp.int32, (tk, tk), 1)).astype(BF16)
    head_spec = lambda part: pl.BlockSpec((seq_len, width),
                                          lambda b, h: (b, part * head_groups + h))
    w_slot = pltpu.VMEM((ATTN_PAIR, ATTN_HEADS, tq, tk), F32)
    sp_slot = pltpu.VMEM((ATTN_PAIR, ATTN_HEADS, tq, tk), BF16)
    return pl.pallas_call(
        functools.partial(_attn_kernel, head_dim=head_dim),
        name="stickbreak_attn",
        grid=(batch, head_groups),
        in_specs=[head_spec(0), head_spec(1), head_spec(2),
                  pl.BlockSpec((tk, tk), lambda b, h: (0, 0))],
        out_specs=pl.BlockSpec((seq_len, width), lambda b, h: (b, h)),
        out_shape=jax.ShapeDtypeStruct((m, N_HEADS * head_dim), BF16),
        scratch_shapes=[pltpu.VMEM((ATTN_HEADS, tq, head_dim), F32),
                        pltpu.VMEM((ATTN_HEADS, tq, 1), F32),
                        w_slot, w_slot, sp_slot, sp_slot],
        compiler_params=_params(("arbitrary", "arbitrary")),
    )(qkv, qkv, qkv, tri)


def kernel(x, ln_g, ln_b, ffn_w_gate, ffn_w_up, ffn_w_down, mix_w_in, pool_w, pool_scale,
           conv_w, mix_w_out, attn_w_qkv, attn_w_out):
    batch, seq_len, d = x.shape
    assert ln_g.shape[0] == DEPTH
    h = x.reshape(batch * seq_len, d)
    row = lambda v: v.reshape(1, -1)

    def ffn(h, layer, idx, ln_idx):
        return _ffn_ln(h, ffn_w_gate, ffn_w_up, ffn_w_down, layer, idx,
                       row(ln_g[layer, ln_idx]), row(ln_b[layer, ln_idx]))

    for layer in range(DEPTH):
        h = ffn(h, layer, 0, 0)
        i = layer // 2
        g, b = row(ln_g[layer, 1]), row(ln_b[layer, 1])
        if layer % 2 == 0:
            y_pool, y_conv = _mixer_front(h, mix_w_in[i], pool_w[i], row(pool_scale[i]),
                                          conv_w[i], seq_len)
            h = _proj_ln(y_pool, y_conv, 0, 0, mix_w_out[i], h, g, b)
        else:
            head_dim = d // N_HEADS
            qkv = _qkv_proj(h, attn_w_qkv[i], LOG2E / math.sqrt(head_dim))
            o = _attention(qkv, batch, seq_len, head_dim)
            h = _proj_ln(o, o, 0, 1, attn_w_out[i], h, g, b)
        h = ffn(h, layer, 1, 2)
    return h.reshape(batch, seq_len, d)
```

```python
import functools
import math

import jax
import jax.numpy as jnp
from jax import lax
from jax.experimental import pallas as pl
from jax.experimental.pallas import tpu as pltpu

F32 = jnp.float32
BF16 = jnp.bfloat16

DEPTH = 2
LN_EPS = 1e-5
DEEPNORM_ALPHA = (2.0 * DEPTH) ** 0.25
N_HEADS = 16
POOL_WINDOWS = (2, 4, 8, 16)
CONV_K = 3

V7X_VMEM_BYTES = 64 * 1024 * 1024
VMEM_LIMIT = V7X_VMEM_BYTES - 6 * 1024 * 1024

ROW_TILE = 1024
FF_TILE = 512
LN_ROWS = 256
PROJ_K_TILE = 512
PROJ_ROW_TILE = 1024
QKV_COL_TILE = 1024
ATTN_Q_TILE = 512
ATTN_K_TILE = 256
ATTN_HEADS = 4
ATTN_PAIR = 2
LOG2E = 1.4426950408889634
MASKED_LOGIT = -1e30
POOL_HALO = 32
CONV_HALO = 8


def _params(semantics):
    return pltpu.CompilerParams(dimension_semantics=semantics, vmem_limit_bytes=VMEM_LIMIT)


def _layer_norm_rows(y, g, b, eps):
    mu = jnp.mean(y, axis=-1, keepdims=True)
    yc = y - mu
    var = jnp.mean(yc * yc, axis=-1, keepdims=True)
    return yc * lax.rsqrt(var + eps) * g + b


class _ResidualTiles:
    def __init__(self, x_hbm, o_hbm, acc_ref, x_sem, o_sem, branch_scale):
        self.x_hbm, self.o_hbm, self.acc_ref = x_hbm, o_hbm, acc_ref
        self.x_sem, self.o_sem = x_sem, o_sem
        self.branch_scale = branch_scale
        self.rows = acc_ref.shape[1]
        self.tile = pl.program_id(0)
        self.slot = self.tile % 2
        self.acc = acc_ref.at[self.slot]

    def _x_copy(self, tile, slot):
        return pltpu.make_async_copy(self.x_hbm.at[pl.ds(tile * self.rows, self.rows)],
                                     self.acc_ref.at[slot], self.x_sem.at[slot])

    def _o_copy(self, tile, slot):
        return pltpu.make_async_copy(self.acc_ref.at[slot],
                                     self.o_hbm.at[pl.ds(tile * self.rows, self.rows)],
                                     self.o_sem.at[slot])

    def begin_step(self, step, xb_ref):
        tile, slot, other = self.tile, self.slot, 1 - self.slot

        @pl.when(jnp.logical_and(tile == 0, step == 0))
        def _():
            self._x_copy(0, 0).start()

        @pl.when(step == 0)
        def _():
            self._x_copy(tile, slot).wait()
            for r in range(0, self.rows, LN_ROWS):
                rows = pl.ds(r, LN_ROWS)
                x = self.acc[rows, :]
                if xb_ref is not None:
                    xb_ref[rows, :] = x.astype(BF16)
                self.acc[rows, :] = (DEEPNORM_ALPHA / self.branch_scale) * x

        @pl.when(step == 1)
        def _():
            @pl.when(tile >= 1)
            def _():
                self._o_copy(tile - 1, other).wait()

            @pl.when(tile + 1 < pl.num_programs(0))
            def _():
                self._x_copy(tile + 1, other).start()

    def finish_tile(self, g_ref, b_ref, add_rows):
        g = g_ref[...]
        b = b_ref[...]
        eps = LN_EPS / (self.branch_scale * self.branch_scale)
        for r in range(0, self.rows, LN_ROWS):
            rows = pl.ds(r, LN_ROWS)
            self.acc[rows, :] = _layer_norm_rows(self.acc[rows, :] + add_rows(r), g, b, eps)
        self._o_copy(self.tile, self.slot).start()

        @pl.when(self.tile == pl.num_programs(0) - 1)
        def _():
            self._o_copy(self.tile, self.slot).wait()


def _ffn_kernel(x_hbm, wg_ref, wu_ref, wd_ref, g_ref, b_ref, o_hbm,
                acc_ref, xb_ref, x_sem, o_sem):
    j = pl.program_id(1)
    tiles = _ResidualTiles(x_hbm, o_hbm, acc_ref, x_sem, o_sem, 0.5)
    tiles.begin_step(j, xb_ref)

    def hidden():
        xb = xb_ref[...]
        gate = jnp.dot(xb, wg_ref[...].astype(BF16), preferred_element_type=F32)
        up = jnp.dot(xb, wu_ref[...].astype(BF16), preferred_element_type=F32)
        return (gate * jax.nn.sigmoid(gate) * up).astype(BF16), wd_ref[...].astype(BF16)

    last = pl.num_programs(1) - 1

    @pl.when(j < last)
    def _():
        h, wd = hidden()
        tiles.acc[...] += jnp.dot(h, wd, preferred_element_type=F32)

    @pl.when(j == last)
    def _():
        h, wd = hidden()
        tiles.finish_tile(g_ref, b_ref, lambda r: jnp.dot(
            h[r:r + LN_ROWS], wd, preferred_element_type=F32))


def _ffn_ln(x, wg, wu, wd, layer, idx, g, b):
    m, d = x.shape
    f = wg.shape[-1]
    grid = (m // ROW_TILE, f // FF_TILE)
    return pl.pallas_call(
        _ffn_kernel,
        name="ffn_ln",
        grid=grid,
        in_specs=[
            pl.BlockSpec(memory_space=pl.ANY),
            pl.BlockSpec((None, None, d, FF_TILE), lambda i, j: (layer, idx, 0, j)),
            pl.BlockSpec((None, None, d, FF_TILE), lambda i, j: (layer, idx, 0, j)),
            pl.BlockSpec((None, None, FF_TILE, d), lambda i, j: (layer, idx, j, 0)),
            pl.BlockSpec((1, d), lambda i, j: (0, 0)),
            pl.BlockSpec((1, d), lambda i, j: (0, 0)),
        ],
        out_specs=pl.BlockSpec(memory_space=pl.ANY),
        out_shape=jax.ShapeDtypeStruct((m, d), F32),
        scratch_shapes=[pltpu.VMEM((2, ROW_TILE, d), F32),
                        pltpu.VMEM((ROW_TILE, d), BF16),
                        pltpu.SemaphoreType.DMA((2,)),
                        pltpu.SemaphoreType.DMA((2,))],
        compiler_params=_params(("arbitrary", "arbitrary")),
    )(x, wg, wu, wd, g, b)


def _proj_kernel(a0_ref, a1_ref, w_ref, x_hbm, g_ref, b_ref, o_hbm,
                 acc_ref, wb_ref, x_sem, o_sem, *, half_steps):
    k = pl.program_id(1)
    tiles = _ResidualTiles(x_hbm, o_hbm, acc_ref, x_sem, o_sem, 1.0)
    tiles.begin_step(k, None)
    w_rows = pl.ds(pl.multiple_of(k * w_ref.shape[0], w_ref.shape[0]), w_ref.shape[0])

    @pl.when(pl.program_id(0) == 0)
    def _():
        wb_ref[w_rows, :] = w_ref[...].astype(BF16)

    w = wb_ref[w_rows, :]

    last = pl.num_programs(1) - 1

    @pl.when(k < half_steps)
    def _():
        tiles.acc[...] += jnp.dot(a0_ref[...], w, preferred_element_type=F32)

    @pl.when(jnp.logical_and(k >= half_steps, k < last))
    def _():
        tiles.acc[...] += jnp.dot(a1_ref[...], w, preferred_element_type=F32)

    @pl.when(k == last)
    def _():
        tiles.finish_tile(g_ref, b_ref, lambda r: jnp.dot(
            a1_ref[pl.ds(r, LN_ROWS), :], w, preferred_element_type=F32))


def _proj_ln(a0, a1, col0, col1, w, x, g, b):
    m, d = x.shape
    kdim = w.shape[0]
    half = kdim // 2
    half_steps = half // PROJ_K_TILE
    k_steps = kdim // PROJ_K_TILE
    grid = (m // PROJ_ROW_TILE, k_steps)
    a0_map = lambda i, k: (i, col0 * half_steps + jnp.minimum(k, half_steps - 1))
    a1_map = lambda i, k: (i, col1 * half_steps + jnp.maximum(k - half_steps, 0))
    w_map = lambda i, k: (jnp.where(i == 0, k, k_steps - 1), 0)
    return pl.pallas_call(
        functools.partial(_proj_kernel, half_steps=half_steps),
        name="proj_ln",
        grid=grid,
        in_specs=[
            pl.BlockSpec((PROJ_ROW_TILE, PROJ_K_TILE), a0_map),
            pl.BlockSpec((PROJ_ROW_TILE, PROJ_K_TILE), a1_map),
            pl.BlockSpec((PROJ_K_TILE, d), w_map),
            pl.BlockSpec(memory_space=pl.ANY),
            pl.BlockSpec((1, d), lambda i, k: (0, 0)),
            pl.BlockSpec((1, d), lambda i, k: (0, 0)),
        ],
        out_specs=pl.BlockSpec(memory_space=pl.ANY),
        out_shape=jax.ShapeDtypeStruct((m, d), F32),
        scratch_shapes=[pltpu.VMEM((2, PROJ_ROW_TILE, d), F32),
                        pltpu.VMEM((kdim, d), BF16),
                        pltpu.SemaphoreType.DMA((2,)),
                        pltpu.SemaphoreType.DMA((2,))],
        compiler_params=_params(("arbitrary", "arbitrary")),
    )(a0, a1, w, x, g, b)


def _mixer_kernel(x_ref, wp_ref, wb_ref, wc_ref, wx_ref, pw_ref, ps_ref, cw_ref,
                  yp_ref, yc_ref, xb_ref, pool_ref, conv_ref, pcarry_ref, ccarry_ref,
                  *, tiles_per_seq):
    i = pl.program_id(0)
    j = pl.program_id(1)
    tm = x_ref.shape[0]
    seq_tile = i % tiles_per_seq

    @pl.when(j == 0)
    def _():
        xb_ref[...] = x_ref[...].astype(BF16)

    @pl.when(seq_tile == 0)
    def _():
        pcarry_ref[j] = jnp.zeros(pcarry_ref.shape[1:], F32)
        ccarry_ref[j] = jnp.zeros(ccarry_ref.shape[1:], F32)

    xb = xb_ref[...]
    u = jnp.dot(xb, wp_ref[...].astype(BF16), preferred_element_type=F32)
    gate_c = jnp.dot(xb, wc_ref[...].astype(BF16), preferred_element_type=F32)
    x_conv = jnp.dot(xb, wx_ref[...].astype(BF16), preferred_element_type=F32)
    gate_b = jnp.dot(xb, wb_ref[...].astype(BF16), preferred_element_type=F32)

    base = 8 + POOL_HALO
    pool_ref[pl.ds(8, POOL_HALO), :] = pcarry_ref[j]
    pool_ref[pl.ds(base, tm), :] = u
    pcarry_ref[j] = u[tm - POOL_HALO:, :]
    for step in range(len(POOL_WINDOWS)):
        shift = 1 << step
        lo = 16 + 8 * step
        n = base + tm - lo
        take = (j >= step).astype(F32)
        pool_ref[pl.ds(lo, n), :] = (pool_ref[pl.ds(lo, n), :]
                                     + take * pool_ref[pl.ds(lo - shift, n), :])

    window = lax.shift_left(jnp.int32(2), j)
    pos = seq_tile * tm + lax.broadcasted_iota(jnp.int32, (tm, 1), 0)
    count = jnp.minimum(pos + 1, window).astype(F32)
    d = pool_ref[pl.ds(base, tm), :] / count - u
    y_pool = jnp.dot(d.astype(BF16), pw_ref[0].astype(BF16), preferred_element_type=F32)
    yp_ref[...] = (y_pool * ps_ref[...]).astype(yp_ref.dtype)

    z = gate_c * x_conv
    cbase = CONV_HALO
    conv_ref[pl.ds(0, CONV_HALO), :] = ccarry_ref[j]
    conv_ref[pl.ds(cbase, tm), :] = z
    ccarry_ref[j] = z[tm - CONV_HALO:, :]
    cw = cw_ref[...]
    y = z * cw[CONV_K - 1:CONV_K, :]
    for tap in range(CONV_K - 1):
        back = CONV_K - 1 - tap
        y = y + conv_ref[pl.ds(cbase - back, tm), :] * cw[tap:tap + 1, :]
    yc_ref[...] = (gate_b * y).astype(yc_ref.dtype)


def _mixer_front(x, w_in, pool_w, pool_scale, conv_w, seq_len):
    m, d = x.shape
    n_groups, group, _ = pool_w.shape
    pool_width = n_groups * group
    conv_width = conv_w.shape[1]
    assert conv_width == pool_width and w_in.shape[1] == pool_width + 3 * conv_width
    assert tuple(2 << g for g in range(n_groups)) == POOL_WINDOWS
    assert POOL_HALO >= POOL_WINDOWS[-1] - 1 and seq_len % ROW_TILE == 0
    chunk = group
    nb = pool_width // chunk
    grid = (m // ROW_TILE, n_groups)
    w_spec = lambda seg: pl.BlockSpec((d, chunk), lambda i, j: (0, seg * nb + j))
    out_spec = pl.BlockSpec((ROW_TILE, chunk), lambda i, j: (i, j))
    return pl.pallas_call(
        functools.partial(_mixer_kernel, tiles_per_seq=seq_len // ROW_TILE),
        name="mixer_front",
        grid=grid,
        in_specs=[
            pl.BlockSpec((ROW_TILE, d), lambda i, j: (i, 0)),
            w_spec(0), w_spec(1), w_spec(2), w_spec(3),
            pl.BlockSpec((1, group, group), lambda i, j: (j, 0, 0)),
            pl.BlockSpec((1, chunk), lambda i, j: (0, j)),
            pl.BlockSpec((CONV_K, chunk), lambda i, j: (0, j)),
        ],
        out_specs=[out_spec, out_spec],
        out_shape=[jax.ShapeDtypeStruct((m, pool_width), BF16),
                   jax.ShapeDtypeStruct((m, conv_width), BF16)],
        scratch_shapes=[
            pltpu.VMEM((ROW_TILE, d), BF16),
            pltpu.VMEM((8 + POOL_HALO + ROW_TILE, chunk), F32),
            pltpu.VMEM((CONV_HALO + ROW_TILE, chunk), F32),
            pltpu.VMEM((n_groups, POOL_HALO, chunk), F32),
            pltpu.VMEM((n_groups, CONV_HALO, chunk), F32),
        ],
        compiler_params=_params(("arbitrary", "arbitrary")),
    )(x, w_in, w_in, w_in, w_in, pool_w, pool_scale, conv_w)


def _qkv_kernel(x_ref, w_ref, o_ref, xb_ref, *, q_tiles, q_scale):
    j = pl.program_id(1)

    @pl.when(j == 0)
    def _():
        xb_ref[...] = x_ref[...].astype(BF16)

    y = jnp.dot(xb_ref[...], w_ref[...].astype(BF16), preferred_element_type=F32)
    col_scale = jnp.where(j < q_tiles, q_scale, 1.0).astype(F32)
    o_ref[...] = (y * col_scale).astype(o_ref.dtype)


def _qkv_proj(x, w, q_scale):
    m, d = x.shape
    n = w.shape[1]
    assert d % QKV_COL_TILE == 0
    grid = (m // ROW_TILE, n // QKV_COL_TILE)
    return pl.pallas_call(
        functools.partial(_qkv_kernel, q_tiles=d // QKV_COL_TILE, q_scale=q_scale),
        name="qkv_proj",
        grid=grid,
        in_specs=[
            pl.BlockSpec((ROW_TILE, d), lambda i, j: (i, 0)),
            pl.BlockSpec((d, QKV_COL_TILE), lambda i, j: (0, j)),
        ],
        out_specs=pl.BlockSpec((ROW_TILE, QKV_COL_TILE), lambda i, j: (i, j)),
        out_shape=jax.ShapeDtypeStruct((m, n), BF16),
        scratch_shapes=[pltpu.VMEM((ROW_TILE, d), BF16)],
        compiler_params=_params(("arbitrary", "arbitrary")),
    )(x, w)


def _attn_kernel(q_ref, k_ref, v_ref, tri_ref, o_ref, acc_ref, carry_ref,
                 w0_ref, w1_ref, sp0_ref, sp1_ref, *, head_dim):
    seq = q_ref.shape[0]
    heads = q_ref.shape[1] // head_dim
    tq, tk = ATTN_Q_TILE, ATTN_K_TILE
    assert tq == ATTN_PAIR * tk
    contract_last = (((1,), (1,)), ((), ()))
    lanes = lambda g: slice(g * head_dim, (g + 1) * head_dim)
    w_slots = (w0_ref, w1_ref)
    sp_slots = (sp0_ref, sp1_ref)

    pair = [(j, g) for j in range(ATTN_PAIR) for g in range(heads)]
    key_rows = lambda kb_first, j: pl.ds(pl.multiple_of((kb_first - j) * tk, tk), tk)

    def score_dot(slot, j, g, q_start, kb_first):
        q = q_ref[pl.ds(q_start, tq), lanes(g)]
        w_slots[slot][j, g] = lax.dot_general(
            q, k_ref[key_rows(kb_first, j), lanes(g)], contract_last,
            preferred_element_type=F32)

    def softplus_one(slot, j, g, q_start, kb_first, masked):
        w_ref = w_slots[slot]
        w = w_ref[j, g]
        sp = jnp.maximum(w, 0.0) + jnp.log(1.0 + jnp.exp2(-jnp.abs(w))) * LOG2E
        if masked:
            col = (kb_first - j) * tk + lax.broadcasted_iota(jnp.int32, (tq, tk), 1)
            row = q_start + lax.broadcasted_iota(jnp.int32, (tq, tk), 0)
            keep = col < row
            sp = jnp.where(keep, sp, 0.0)
            w_ref[j, g] = jnp.where(keep, w, MASKED_LOGIT)
        sp_slots[slot][j, g] = sp.astype(BF16)

    def suffix_dot(slot, j, g):
        return jnp.dot(sp_slots[slot][j, g], tri_ref[...], preferred_element_type=F32)

    def accumulate_one(slot, j, g, suffix, kb_first):
        a = jnp.exp2(w_slots[slot][j, g] - suffix - carry_ref[g])
        acc_ref[g] += jnp.dot(a.astype(BF16), v_ref[key_rows(kb_first, j), lanes(g)],
                              preferred_element_type=F32)
        carry_ref[g] += suffix[:, 0:1]

    def first_pair(q_start, kb_first):
        for j, g in pair:
            score_dot(0, j, g, q_start, kb_first)
        for j, g in pair:
            softplus_one(0, j, g, q_start, kb_first, True)

    def last_pair(slot, kb_first):
        suffix_all = [suffix_dot(slot, j, g) for j, g in pair]
        for (j, g), suffix in zip(pair, suffix_all):
            accumulate_one(slot, j, g, suffix, kb_first)

    def pair_step(slot, q_start, kb_first):
        prev = 1 - slot
        pending = None
        for j, g in pair:
            suffix = suffix_dot(prev, j, g)
            score_dot(slot, j, g, q_start, kb_first)
            if pending is not None:
                accumulate_one(prev, *pending, kb_first + ATTN_PAIR)
            pending = (j, g, suffix)
            softplus_one(slot, j, g, q_start, kb_first, False)
        accumulate_one(prev, *pending, kb_first + ATTN_PAIR)

    def q_tile(qt, _):
        q_start = pl.multiple_of(qt * tq, tq)
        last_kb = ATTN_PAIR * qt + ATTN_PAIR - 1
        acc_ref[...] = jnp.zeros(acc_ref.shape, F32)
        carry_ref[...] = jnp.zeros(carry_ref.shape, F32)
        first_pair(q_start, last_kb)

        def off_diagonal_pair(p, _):
            for slot in range(2):
                @pl.when(p % 2 == slot)
                def _():
                    pair_step(slot, q_start, last_kb - ATTN_PAIR * p)
            return 0

        lax.fori_loop(1, qt + 1, off_diagonal_pair, 0)
        for slot in range(2):
            @pl.when(qt % 2 == slot)
            def _():
                last_pair(slot, ATTN_PAIR - 1)
        for g in range(heads):
            o_ref[pl.ds(q_start, tq), lanes(g)] = acc_ref[g].astype(o_ref.dtype)
        return 0

    lax.fori_loop(0, seq // tq, q_tile, 0)


def _attention(qkv, batch, seq_len, head_dim):
    m = qkv.shape[0]
    tq, tk = ATTN_Q_TILE, ATTN_K_TILE
    assert seq_len % tq == 0 and N_HEADS % ATTN_HEADS == 0
    head_groups = N_HEADS // ATTN_HEADS
    width = ATTN_HEADS * head_dim
    tri = (lax.broadcasted_iota(jnp.int32, (tk, tk), 0)
           >= lax.broadcasted_iota(jnp.int32, (tk, tk), 1)).astype(BF16)
    head_spec = lambda part: pl.BlockSpec((seq_len, width),
                                          lambda b, h: (b, part * head_groups + h))
    w_slot = pltpu.VMEM((ATTN_PAIR, ATTN_HEADS, tq, tk), F32)
    sp_slot = pltpu.VMEM((ATTN_PAIR, ATTN_HEADS, tq, tk), BF16)
    return pl.pallas_call(
        functools.partial(_attn_kernel, head_dim=head_dim),
        name="stickbreak_attn",
        grid=(batch, head_groups),
        in_specs=[head_spec(0), head_spec(1), head_spec(2),
                  pl.BlockSpec((tk, tk), lambda b, h: (0, 0))],
        out_specs=pl.BlockSpec((seq_len, width), lambda b, h: (b, h)),
        out_shape=jax.ShapeDtypeStruct((m, N_HEADS * head_dim), BF16),
        scratch_shapes=[pltpu.VMEM((ATTN_HEADS, tq, head_dim), F32),
                        pltpu.VMEM((ATTN_HEADS, tq, 1), F32),
                        w_slot, w_slot, sp_slot, sp_slot],
        compiler_params=_params(("arbitrary", "arbitrary")),
    )(qkv, qkv, qkv, tri)


def kernel(x, ln_g, ln_b, ffn_w_gate, ffn_w_up, ffn_w_down, mix_w_in, pool_w, pool_scale,
           conv_w, mix_w_out, attn_w_qkv, attn_w_out):
    batch, seq_len, d = x.shape
    assert ln_g.shape[0] == DEPTH
    h = x.reshape(batch * seq_len, d)
    row = lambda v: v.reshape(1, -1)

    def ffn(h, layer, idx, ln_idx):
        return _ffn_ln(h, ffn_w_gate, ffn_w_up, ffn_w_down, layer, idx,
                       row(ln_g[layer, ln_idx]), row(ln_b[layer, ln_idx]))

    for layer in range(DEPTH):
        h = ffn(h, layer, 0, 0)
        i = layer // 2
        g, b = row(ln_g[layer, 1]), row(ln_b[layer, 1])
        if layer % 2 == 0:
            y_pool, y_conv = _mixer_front(h, mix_w_in[i], pool_w[i], row(pool_scale[i]),
                                          conv_w[i], seq_len)
            h = _proj_ln(y_pool, y_conv, 0, 0, mix_w_out[i], h, g, b)
        else:
            head_dim = d // N_HEADS
            qkv = _qkv_proj(h, attn_w_qkv[i], LOG2E / math.sqrt(head_dim))
            o = _attention(qkv, batch, seq_len, head_dim)
            h = _proj_ln(o, o, 0, 1, attn_w_out[i], h, g, b)
        h = ffn(h, layer, 1, 2)
    return h.reshape(batch, seq_len, d)
```

```python
import functools
import math

import jax
import jax.numpy as jnp
from jax import lax
from jax.experimental import pallas as pl
from jax.experimental.pallas import tpu as pltpu

F32 = jnp.float32
BF16 = jnp.bfloat16

DEPTH = 2
LN_EPS = 1e-5
DEEPNORM_ALPHA = (2.0 * DEPTH) ** 0.25
N_HEADS = 16
POOL_WINDOWS = (2, 4, 8, 16)
CONV_K = 3

V7X_VMEM_BYTES = 64 * 1024 * 1024
VMEM_LIMIT = V7X_VMEM_BYTES - 6 * 1024 * 1024

ROW_TILE = 1024
FF_TILE = 512
LN_ROWS = 256
PROJ_K_TILE = 512
PROJ_ROW_TILE = 1024
QKV_COL_TILE = 1024
ATTN_Q_TILE = 512
ATTN_K_TILE = 256
ATTN_HEADS = 4
ATTN_PAIR = 2
LOG2E = 1.4426950408889634
MASKED_LOGIT = -1e30
POOL_HALO = 32
CONV_HALO = 8


def _params(semantics):
    return pltpu.CompilerParams(dimension_semantics=semantics, vmem_limit_bytes=VMEM_LIMIT)


def _layer_norm_rows(y, g, b, eps):
    mu = jnp.mean(y, axis=-1, keepdims=True)
    yc = y - mu
    var = jnp.mean(yc * yc, axis=-1, keepdims=True)
    return yc * lax.rsqrt(var + eps) * g + b


class _ResidualTiles:
    def __init__(self, x_hbm, o_hbm, acc_ref, x_sem, o_sem, branch_scale):
        self.x_hbm, self.o_hbm, self.acc_ref = x_hbm, o_hbm, acc_ref
        self.x_sem, self.o_sem = x_sem, o_sem
        self.branch_scale = branch_scale
        self.rows = acc_ref.shape[1]
        self.tile = pl.program_id(0)
        self.slot = self.tile % 2
        self.acc = acc_ref.at[self.slot]

    def _x_copy(self, tile, slot):
        return pltpu.make_async_copy(self.x_hbm.at[pl.ds(tile * self.rows, self.rows)],
                                     self.acc_ref.at[slot], self.x_sem.at[slot])

    def _o_copy(self, tile, slot):
        return pltpu.make_async_copy(self.acc_ref.at[slot],
                                     self.o_hbm.at[pl.ds(tile * self.rows, self.rows)],
                                     self.o_sem.at[slot])

    def begin_step(self, step, xb_ref):
        tile, slot, other = self.tile, self.slot, 1 - self.slot

        @pl.when(jnp.logical_and(tile == 0, step == 0))
        def _():
            self._x_copy(0, 0).start()

        @pl.when(step == 0)
        def _():
            self._x_copy(tile, slot).wait()
            for r in range(0, self.rows, LN_ROWS):
                rows = pl.ds(r, LN_ROWS)
                x = self.acc[rows, :]
                if xb_ref is not None:
                    xb_ref[rows, :] = x.astype(BF16)
                self.acc[rows, :] = (DEEPNORM_ALPHA / self.branch_scale) * x

        @pl.when(step == 1)
        def _():
            @pl.when(tile >= 1)
            def _():
                self._o_copy(tile - 1, other).wait()

            @pl.when(tile + 1 < pl.num_programs(0))
            def _():
                self._x_copy(tile + 1, other).start()

    def finish_tile(self, g_ref, b_ref, add_rows):
        g = g_ref[...]
        b = b_ref[...]
        eps = LN_EPS / (self.branch_scale * self.branch_scale)
        for r in range(0, self.rows, LN_ROWS):
            rows = pl.ds(r, LN_ROWS)
            self.acc[rows, :] = _layer_norm_rows(self.acc[rows, :] + add_rows(r), g, b, eps)
        self._o_copy(self.tile, self.slot).start()

        @pl.when(self.tile == pl.num_programs(0) - 1)
        def _():
            self._o_copy(self.tile, self.slot).wait()


def _ffn_kernel(x_hbm, wg_ref, wu_ref, wd_ref, g_ref, b_ref, o_hbm,
                acc_ref, xb_ref, x_sem, o_sem):
    j = pl.program_id(1)
    tiles = _ResidualTiles(x_hbm, o_hbm, acc_ref, x_sem, o_sem, 0.5)
    tiles.begin_step(j, xb_ref)

    def hidden():
        xb = xb_ref[...]
        gate = jnp.dot(xb, wg_ref[...].astype(BF16), preferred_element_type=F32)
        up = jnp.dot(xb, wu_ref[...].astype(BF16), preferred_element_type=F32)
        return (gate * jax.nn.sigmoid(gate) * up).astype(BF16), wd_ref[...].astype(BF16)

    last = pl.num_programs(1) - 1

    @pl.when(j < last)
    def _():
        h, wd = hidden()
        tiles.acc[...] += jnp.dot(h, wd, preferred_element_type=F32)

    @pl.when(j == last)
    def _():
        h, wd = hidden()
        tiles.finish_tile(g_ref, b_ref, lambda r: jnp.dot(
            h[r:r + LN_ROWS], wd, preferred_element_type=F32))


def _ffn_ln(x, wg, wu, wd, layer, idx, g, b):
    m, d = x.shape
    f = wg.shape[-1]
    grid = (m // ROW_TILE, f // FF_TILE)
    return pl.pallas_call(
        _ffn_kernel,
        name="ffn_ln",
        grid=grid,
        in_specs=[
            pl.BlockSpec(memory_space=pl.ANY),
            pl.BlockSpec((None, None, d, FF_TILE), lambda i, j: (layer, idx, 0, j)),
            pl.BlockSpec((None, None, d, FF_TILE), lambda i, j: (layer, idx, 0, j)),
            pl.BlockSpec((None, None, FF_TILE, d), lambda i, j: (layer, idx, j, 0)),
            pl.BlockSpec((1, d), lambda i, j: (0, 0)),
            pl.BlockSpec((1, d), lambda i, j: (0, 0)),
        ],
        out_specs=pl.BlockSpec(memory_space=pl.ANY),
        out_shape=jax.ShapeDtypeStruct((m, d), F32),
        scratch_shapes=[pltpu.VMEM((2, ROW_TILE, d), F32),
                        pltpu.VMEM((ROW_TILE, d), BF16),
                        pltpu.SemaphoreType.DMA((2,)),
                        pltpu.SemaphoreType.DMA((2,))],
        compiler_params=_params(("arbitrary", "arbitrary")),
    )(x, wg, wu, wd, g, b)


def _proj_kernel(a0_ref, a1_ref, w_ref, x_hbm, g_ref, b_ref, o_hbm,
                 acc_ref, wb_ref, x_sem, o_sem, *, half_steps):
    k = pl.program_id(1)
    tiles = _ResidualTiles(x_hbm, o_hbm, acc_ref, x_sem, o_sem, 1.0)
    tiles.begin_step(k, None)
    w_rows = pl.ds(pl.multiple_of(k * w_ref.shape[0], w_ref.shape[0]), w_ref.shape[0])

    @pl.when(pl.program_id(0) == 0)
    def _():
        wb_ref[w_rows, :] = w_ref[...].astype(BF16)

    w = wb_ref[w_rows, :]

    last = pl.num_programs(1) - 1

    @pl.when(k < half_steps)
    def _():
        tiles.acc[...] += jnp.dot(a0_ref[...], w, preferred_element_type=F32)

    @pl.when(jnp.logical_and(k >= half_steps, k < last))
    def _():
        tiles.acc[...] += jnp.dot(a1_ref[...], w, preferred_element_type=F32)

    @pl.when(k == last)
    def _():
        tiles.finish_tile(g_ref, b_ref, lambda r: jnp.dot(
            a1_ref[pl.ds(r, LN_ROWS), :], w, preferred_element_type=F32))


def _proj_ln(a0, a1, col0, col1, w, x, g, b):
    m, d = x.shape
    kdim = w.shape[0]
    half = kdim // 2
    half_steps = half // PROJ_K_TILE
    k_steps = kdim // PROJ_K_TILE
    grid = (m // PROJ_ROW_TILE, k_steps)
    a0_map = lambda i, k: (i, col0 * half_steps + jnp.minimum(k, half_steps - 1))
    a1_map = lambda i, k: (i, col1 * half_steps + jnp.maximum(k - half_steps, 0))
    w_map = lambda i, k: (jnp.where(i == 0, k, k_steps - 1), 0)
    return pl.pallas_call(
        functools.partial(_proj_kernel, half_steps=half_steps),
        name="proj_ln",
        grid=grid,
        in_specs=[
            pl.BlockSpec((PROJ_ROW_TILE, PROJ_K_TILE), a0_map),
            pl.BlockSpec((PROJ_ROW_TILE, PROJ_K_TILE), a1_map),
            pl.BlockSpec((PROJ_K_TILE, d), w_map),
            pl.BlockSpec(memory_space=pl.ANY),
            pl.BlockSpec((1, d), lambda i, k: (0, 0)),
            pl.BlockSpec((1, d), lambda i, k: (0, 0)),
        ],
        out_specs=pl.BlockSpec(memory_space=pl.ANY),
        out_shape=jax.ShapeDtypeStruct((m, d), F32),
        scratch_shapes=[pltpu.VMEM((2, PROJ_ROW_TILE, d), F32),
                        pltpu.VMEM((kdim, d), BF16),
                        pltpu.SemaphoreType.DMA((2,)),
                        pltpu.SemaphoreType.DMA((2,))],
        compiler_params=_params(("arbitrary", "arbitrary")),
    )(a0, a1, w, x, g, b)


def _mixer_kernel(x_ref, wp_ref, wb_ref, wc_ref, wx_ref, pw_ref, ps_ref, cw_ref,
                  yp_ref, yc_ref, xb_ref, pool_ref, conv_ref, pcarry_ref, ccarry_ref,
                  *, tiles_per_seq):
    i = pl.program_id(0)
    j = pl.program_id(1)
    tm = x_ref.shape[0]
    seq_tile = i % tiles_per_seq

    @pl.when(j == 0)
    def _():
        xb_ref[...] = x_ref[...].astype(BF16)

    @pl.when(seq_tile == 0)
    def _():
        pcarry_ref[j] = jnp.zeros(pcarry_ref.shape[1:], F32)
        ccarry_ref[j] = jnp.zeros(ccarry_ref.shape[1:], F32)

    xb = xb_ref[...]
    u = jnp.dot(xb, wp_ref[...].astype(BF16), preferred_element_type=F32)
    gate_c = jnp.dot(xb, wc_ref[...].astype(BF16), preferred_element_type=F32)
    x_conv = jnp.dot(xb, wx_ref[...].astype(BF16), preferred_element_type=F32)
    gate_b = jnp.dot(xb, wb_ref[...].astype(BF16), preferred_element_type=F32)

    base = 8 + POOL_HALO
    pool_ref[pl.ds(8, POOL_HALO), :] = pcarry_ref[j]
    pool_ref[pl.ds(base, tm), :] = u
    pcarry_ref[j] = u[tm - POOL_HALO:, :]
    for step in range(len(POOL_WINDOWS)):
        shift = 1 << step
        lo = 16 + 8 * step
        n = base + tm - lo
        take = (j >= step).astype(F32)
        pool_ref[pl.ds(lo, n), :] = (pool_ref[pl.ds(lo, n), :]
                                     + take * pool_ref[pl.ds(lo - shift, n), :])

    window = lax.shift_left(jnp.int32(2), j)
    pos = seq_tile * tm + lax.broadcasted_iota(jnp.int32, (tm, 1), 0)
    count = jnp.minimum(pos + 1, window).astype(F32)
    d = pool_ref[pl.ds(base, tm), :] / count - u
    y_pool = jnp.dot(d.astype(BF16), pw_ref[0].astype(BF16), preferred_element_type=F32)
    yp_ref[...] = (y_pool * ps_ref[...]).astype(yp_ref.dtype)

    z = gate_c * x_conv
    cbase = CONV_HALO
    conv_ref[pl.ds(0, CONV_HALO), :] = ccarry_ref[j]
    conv_ref[pl.ds(cbase, tm), :] = z
    ccarry_ref[j] = z[tm - CONV_HALO:, :]
    cw = cw_ref[...]
    y = z * cw[CONV_K - 1:CONV_K, :]
    for tap in range(CONV_K - 1):
        back = CONV_K - 1 - tap
        y = y + conv_ref[pl.ds(cbase - back, tm), :] * cw[tap:tap + 1, :]
    yc_ref[...] = (gate_b * y).astype(yc_ref.dtype)


def _mixer_front(x, w_in, pool_w, pool_scale, conv_w, seq_len):
    m, d = x.shape
    n_groups, group, _ = pool_w.shape
    pool_width = n_groups * group
    conv_width = conv_w.shape[1]
    assert conv_width == pool_width and w_in.shape[1] == pool_width + 3 * conv_width
    assert tuple(2 << g for g in range(n_groups)) == POOL_WINDOWS
    assert POOL_HALO >= POOL_WINDOWS[-1] - 1 and seq_len % ROW_TILE == 0
    chunk = group
    nb = pool_width // chunk
    grid = (m // ROW_TILE, n_groups)
    w_spec = lambda seg: pl.BlockSpec((d, chunk), lambda i, j: (0, seg * nb + j))
    out_spec = pl.BlockSpec((ROW_TILE, chunk), lambda i, j: (i, j))
    return pl.pallas_call(
        functools.partial(_mixer_kernel, tiles_per_seq=seq_len // ROW_TILE),
        name="mixer_front",
        grid=grid,
        in_specs=[
            pl.BlockSpec((ROW_TILE, d), lambda i, j: (i, 0)),
            w_spec(0), w_spec(1), w_spec(2), w_spec(3),
            pl.BlockSpec((1, group, group), lambda i, j: (j, 0, 0)),
            pl.BlockSpec((1, chunk), lambda i, j: (0, j)),
            pl.BlockSpec((CONV_K, chunk), lambda i, j: (0, j)),
        ],
        out_specs=[out_spec, out_spec],
        out_shape=[jax.ShapeDtypeStruct((m, pool_width), BF16),
                   jax.ShapeDtypeStruct((m, conv_width), BF16)],
        scratch_shapes=[
            pltpu.VMEM((ROW_TILE, d), BF16),
            pltpu.VMEM((8 + POOL_HALO + ROW_TILE, chunk), F32),
            pltpu.VMEM((CONV_HALO + ROW_TILE, chunk), F32),
            pltpu.VMEM((n_groups, POOL_HALO, chunk), F32),
            pltpu.VMEM((n_groups, CONV_HALO, chunk), F32),
        ],
        compiler_params=_params(("arbitrary", "arbitrary")),
    )(x, w_in, w_in, w_in, w_in, pool_w, pool_scale, conv_w)


def _qkv_kernel(x_ref, w_ref, o_ref, xb_ref, *, q_tiles, q_scale):
    j = pl.program_id(1)

    @pl.when(j == 0)
    def _():
        xb_ref[...] = x_ref[...].astype(BF16)

    y = jnp.dot(xb_ref[...], w_ref[...].astype(BF16), preferred_element_type=F32)
    col_scale = jnp.where(j < q_tiles, q_scale, 1.0).astype(F32)
    o_ref[...] = (y * col_scale).astype(o_ref.dtype)


def _qkv_proj(x, w, q_scale):
    m, d = x.shape
    n = w.shape[1]
    assert d % QKV_COL_TILE == 0
    grid = (m // ROW_TILE, n // QKV_COL_TILE)
    return pl.pallas_call(
        functools.partial(_qkv_kernel, q_tiles=d // QKV_COL_TILE, q_scale=q_scale),
        name="qkv_proj",
        grid=grid,
        in_specs=[
            pl.BlockSpec((ROW_TILE, d), lambda i, j: (i, 0)),
            pl.BlockSpec((d, QKV_COL_TILE), lambda i, j: (0, j)),
        ],
        out_specs=pl.BlockSpec((ROW_TILE, QKV_COL_TILE), lambda i, j: (i, j)),
        out_shape=jax.ShapeDtypeStruct((m, n), BF16),
        scratch_shapes=[pltpu.VMEM((ROW_TILE, d), BF16)],
        compiler_params=_params(("arbitrary", "arbitrary")),
    )(x, w)


def _attn_kernel(q_ref, k_ref, v_ref, tri_ref, o_ref, acc_ref, carry_ref,
                 w0_ref, w1_ref, wd_ref, sp0_ref, sp1_ref, spd_ref, *, head_dim):
    seq = q_ref.shape[0]
    heads = q_ref.shape[1] // head_dim
    tq, tk = ATTN_Q_TILE, ATTN_K_TILE
    n_tiles = seq // tq
    assert tq == ATTN_PAIR * tk == 2 * tk and n_tiles >= 2
    contract_last = (((1,), (1,)), ((), ()))
    lanes = lambda g: slice(g * head_dim, (g + 1) * head_dim)
    plain = ((w0_ref, sp0_ref), (w1_ref, sp1_ref))
    diag = (wd_ref, spd_ref)

    pair = [(j, g) for j in range(ATTN_PAIR) for g in range(heads)]
    key_rows = lambda kb_first, j: pl.ds(pl.multiple_of((kb_first - j) * tk, tk), tk)
    row0 = lambda buf, j: tk if (buf is diag and j == 0) else 0

    def score_dot(buf, j, g, q_start, kb_first):
        r0 = row0(buf, j)
        q = q_ref[pl.ds(q_start + r0, tq - r0), lanes(g)]
        buf[0][j, g, r0:, :] = lax.dot_general(
            q, k_ref[key_rows(kb_first, j), lanes(g)], contract_last,
            preferred_element_type=F32)

    def softplus_one(buf, j, g, q_start, kb_first):
        r0 = row0(buf, j)
        w = buf[0][j, g, r0:, :]
        sp = jnp.maximum(w, 0.0) + jnp.log(1.0 + jnp.exp2(-jnp.abs(w))) * LOG2E
        if buf is diag:
            col = (kb_first - j) * tk + lax.broadcasted_iota(jnp.int32, (tq - r0, tk), 1)
            row = q_start + r0 + lax.broadcasted_iota(jnp.int32, (tq - r0, tk), 0)
            keep = col < row
            sp = jnp.where(keep, sp, 0.0)
            buf[0][j, g, r0:, :] = jnp.where(keep, w, MASKED_LOGIT)
        buf[1][j, g, r0:, :] = sp.astype(BF16)

    def suffix_dot(buf, j, g):
        return jnp.dot(buf[1][j, g, row0(buf, j):, :], tri_ref[...],
                       preferred_element_type=F32)

    def accumulate_one(buf, j, g, suffix, kb_first):
        r0 = row0(buf, j)
        a = jnp.exp2(buf[0][j, g, r0:, :] - suffix - carry_ref[g, r0:, :])
        acc_ref[g, r0:, :] += jnp.dot(a.astype(BF16), v_ref[key_rows(kb_first, j), lanes(g)],
                                      preferred_element_type=F32)
        carry_ref[g, r0:, :] += suffix[:, 0:1]

    def scores_only(buf, q_start, kb_first):
        for j, g in pair:
            score_dot(buf, j, g, q_start, kb_first)
        for j, g in pair:
            softplus_one(buf, j, g, q_start, kb_first)

    def weights_only(buf, kb_first):
        suffix_all = [suffix_dot(buf, j, g) for j, g in pair]
        for (j, g), suffix in zip(pair, suffix_all):
            accumulate_one(buf, j, g, suffix, kb_first)

    def step(cur, q_start, kb_cur, prev, kb_prev):
        pending = None
        for j, g in pair:
            suffix = suffix_dot(prev, j, g)
            score_dot(cur, j, g, q_start, kb_cur)
            if pending is not None:
                accumulate_one(prev, *pending, kb_prev)
            pending = (j, g, suffix)
            softplus_one(cur, j, g, q_start, kb_cur)
        accumulate_one(prev, *pending, kb_prev)

    def finish_tile(q_start):
        for g in range(heads):
            o_ref[pl.ds(q_start, tq), lanes(g)] = acc_ref[g].astype(o_ref.dtype)
        acc_ref[...] = jnp.zeros(acc_ref.shape, F32)
        carry_ref[...] = jnp.zeros(carry_ref.shape, F32)

    def for_parity(value, body):
        for parity in range(2):
            @pl.when(value % 2 == parity)
            def _():
                body(parity)

    def q_tile(qt, _):
        q_start = pl.multiple_of(qt * tq, tq)
        next_start = pl.multiple_of(qt * tq + tq, tq)
        last_kb = 2 * qt + 1
        last_pair_kb = 1

        @pl.when(qt >= 1)
        def _():
            step(plain[1], q_start, last_kb - 2, diag, last_kb)

        def off_diagonal_pair(p, _):
            for_parity(p, lambda s: step(plain[s], q_start, last_kb - 2 * p,
                                         plain[1 - s], last_kb - 2 * p + 2))
            return 0

        lax.fori_loop(2, qt + 1, off_diagonal_pair, 0)

        @pl.when(qt == 0)
        def _():
            weights_only(diag, last_pair_kb)
            finish_tile(q_start)
            scores_only(diag, next_start, last_kb + 2)

        @pl.when(jnp.logical_and(qt >= 1, qt < n_tiles - 1))
        def _():
            def tail(s):
                step(diag, next_start, last_kb + 2, plain[s], last_pair_kb)
                finish_tile(q_start)
            for_parity(qt, tail)

        @pl.when(qt == n_tiles - 1)
        def _():
            def tail(s):
                weights_only(plain[s], last_pair_kb)
                finish_tile(q_start)
            for_parity(qt, tail)
        return 0

    acc_ref[...] = jnp.zeros(acc_ref.shape, F32)
    carry_ref[...] = jnp.zeros(carry_ref.shape, F32)
    scores_only(diag, 0, 1)
    lax.fori_loop(0, n_tiles, q_tile, 0)


def _attention(qkv, batch, seq_len, head_dim):
    m = qkv.shape[0]
    tq, tk = ATTN_Q_TILE, ATTN_K_TILE
    assert seq_len % tq == 0 and N_HEADS % ATTN_HEADS == 0
    head_groups = N_HEADS // ATTN_HEADS
    width = ATTN_HEADS * head_dim
    tri = (lax.broadcasted_iota(jnp.int32, (tk, tk), 0)
           >= lax.broadcasted_iota(jnp.int32, (tk, tk), 1)).astype(BF16)
    head_spec = lambda part: pl.BlockSpec((seq_len, width),
                                          lambda b, h: (b, part * head_groups + h))
    w_slot = pltpu.VMEM((ATTN_PAIR, ATTN_HEADS, tq, tk), F32)
    sp_slot = pltpu.VMEM((ATTN_PAIR, ATTN_HEADS, tq, tk), BF16)
    return pl.pallas_call(
        functools.partial(_attn_kernel, head_dim=head_dim),
        name="stickbreak_attn",
        grid=(batch, head_groups),
        in_specs=[head_spec(0), head_spec(1), head_spec(2),
                  pl.BlockSpec((tk, tk), lambda b, h: (0, 0))],
        out_specs=pl.BlockSpec((seq_len, width), lambda b, h: (b, h)),
        out_shape=jax.ShapeDtypeStruct((m, N_HEADS * head_dim), BF16),
        scratch_shapes=[pltpu.VMEM((ATTN_HEADS, tq, head_dim), F32),
                        pltpu.VMEM((ATTN_HEADS, tq, 1), F32),
                        w_slot, w_slot, w_slot, sp_slot, sp_slot, sp_slot],
        compiler_params=_params(("arbitrary", "arbitrary")),
    )(qkv, qkv, qkv, tri)


def kernel(x, ln_g, ln_b, ffn_w_gate, ffn_w_up, ffn_w_down, mix_w_in, pool_w, pool_scale,
           conv_w, mix_w_out, attn_w_qkv, attn_w_out):
    batch, seq_len, d = x.shape
    assert ln_g.shape[0] == DEPTH
    h = x.reshape(batch * seq_len, d)
    row = lambda v: v.reshape(1, -1)

    def ffn(h, layer, idx, ln_idx):
        return _ffn_ln(h, ffn_w_gate, ffn_w_up, ffn_w_down, layer, idx,
                       row(ln_g[layer, ln_idx]), row(ln_b[layer, ln_idx]))

    for layer in range(DEPTH):
        h = ffn(h, layer, 0, 0)
        i = layer // 2
        g, b = row(ln_g[layer, 1]), row(ln_b[layer, 1])
        if layer % 2 == 0:
            y_pool, y_conv = _mixer_front(h, mix_w_in[i], pool_w[i], row(pool_scale[i]),
                                          conv_w[i], seq_len)
            h = _proj_ln(y_pool, y_conv, 0, 0, mix_w_out[i], h, g, b)
        else:
            head_dim = d // N_HEADS
            qkv = _qkv_proj(h, attn_w_qkv[i], LOG2E / math.sqrt(head_dim))
            o = _attention(qkv, batch, seq_len, head_dim)
            h = _proj_ln(o, o, 0, 1, attn_w_out[i], h, g, b)
        h = ffn(h, layer, 1, 2)
    return h.reshape(batch, seq_len, d)
```
